```python
import math
import jax, jax.numpy as jnp
from jax import lax
import numpy as np

D_MODEL = 1024
BATCH = 2
SEQ = 8192
DEPTH = 2
DEC_BATCH = 128
DEC_SEQ = 4
PAST_LEN = 16384
PAGE_SIZE = 128

N_A_LAYERS = DEPTH // 2
N_B_LAYERS = DEPTH - N_A_LAYERS
D_FF = 2816
GLA_HEADS = 4
GLA_DK = 96
GLA_DV = 192
GLA_QK = GLA_HEADS * GLA_DK
GLA_V = GLA_HEADS * GLA_DV
GLA_GATE_RANK = 16
GLA_GATE_TEMP = 16.0
GLA_CHUNK = 64
MEM_TOKENS = 256
MEM_HEADS = 4
MEM_DH = 64
MEM_W = MEM_HEADS * MEM_DH
MLA_HEADS = 12
MLA_DN = 64
MLA_DR = 32
MLA_DV = 64
MLA_KV_RANK = 256
MLA_Q_RANK = 384
MLA_V = MLA_HEADS * MLA_DV
ROPE_THETA = 10000.0
Q_BLOCK = 128
A_IN = 2 * GLA_QK + GLA_V + GLA_GATE_RANK + GLA_V + MEM_W
B_IN = MLA_Q_RANK + MEM_W
MIX_OUT = GLA_V + MEM_W
EPS = 1e-6

kernel_name = "yoco_gla_mla_macaron_memory_step"


def rms_norm(x, g):
    xf = x.astype(jnp.float32)
    y = xf * lax.rsqrt(jnp.mean(xf * xf, axis=-1, keepdims=True) + EPS)
    return (y * g.astype(jnp.float32)).astype(x.dtype)


def rope(x, pos):
    half = x.shape[-1] // 2
    inv = ROPE_THETA ** (-jnp.arange(half, dtype=jnp.float32) / half)
    ang = pos.astype(jnp.float32)[:, None] * inv
    ang = ang.reshape((pos.shape[0],) + (1,) * (x.ndim - 3) + (half,))
    cos, sin = jnp.cos(ang), jnp.sin(ang)
    xf = x.astype(jnp.float32)
    x1, x2 = xf[..., :half], xf[..., half:]
    return jnp.concatenate([x1 * cos - x2 * sin, x2 * cos + x1 * sin], axis=-1).astype(x.dtype)


def swiglu_half(x, g, wg, wu, wd):
    h = rms_norm(x, g)
    return x + 0.5 * ((jax.nn.silu(h @ wg) * (h @ wu)) @ wd)


def gla_recurrence(q, k, v, log_a, s0):
    bsz, t, h, dk = q.shape
    dv = v.shape[-1]
    c = math.gcd(t, GLA_CHUNK)
    n = t // c
    f32 = jnp.float32

    def chunks(a):
        return a.astype(f32).reshape(bsz, n, c, h, a.shape[-1]).transpose(1, 0, 3, 2, 4)

    causal = jnp.tril(jnp.ones((c, c), dtype=bool))[:, :, None]

    def step(s, inp):
        qc, kc, vc, ac = inp
        a_cum = jnp.cumsum(ac, axis=2)
        a_last = a_cum[:, :, -1:, :]
        o_inter = jnp.einsum('bhtk,bhkv->bhtv', qc * jnp.exp(a_cum), s)
        decay = jnp.exp(jnp.where(causal, a_cum[:, :, :, None, :] - a_cum[:, :, None, :, :], -jnp.inf))
        scores = jnp.einsum('bhtk,bhtsk,bhsk->bhts', qc, decay, kc)
        o = o_inter + jnp.einsum('bhts,bhsv->bhtv', scores, vc)
        s = jnp.exp(a_last[:, :, 0, :])[..., None] * s + jnp.einsum('bhsk,bhsv->bhkv', kc * jnp.exp(a_last - a_cum), vc)
        return s, o

    s, o = lax.scan(step, s0.astype(f32), (chunks(q), chunks(k), chunks(v), chunks(log_a)))
    return o.transpose(1, 0, 3, 2, 4).reshape(bsz, t, h, dv), s


def mem_kv(mem, g, wk, wv, gk):
    b, m, _ = mem.shape
    mn = rms_norm(mem, g)
    k = rms_norm((mn @ wk).reshape(b, m, MEM_HEADS, MEM_DH), gk)
    v = (mn @ wv).reshape(b, m, MEM_HEADS, MEM_DH)
    return k, v


def mem_attend(q, mk, mv):
    bsz, t = q.shape[:2]
    s = jnp.einsum('bthd,bmhd->bhtm', q, mk).astype(jnp.float32) * (MEM_DH ** -0.5)
    p = jax.nn.softmax(s, axis=-1).astype(mv.dtype)
    return jnp.einsum('bhtm,bmhd->bthd', p, mv).reshape(bsz, t, MEM_W)


def mla_kv(c, w_uk, g_kn, w_uv):
    bsz, l = c.shape[:2]
    kn = rms_norm((c @ w_uk).reshape(bsz, l, MLA_HEADS, MLA_DN), g_kn)
    v = (c @ w_uv).reshape(bsz, l, MLA_HEADS, MLA_DV)
    return kn, v


def mla_attend(qn, qp, kn, kp, v, q_pos, k_pos):
    s = jnp.einsum('bthd,blhd->bhtl', qn, kn) + jnp.einsum('bthr,blr->bhtl', qp, kp)
    s = s.astype(jnp.float32) * ((MLA_DN + MLA_DR) ** -0.5)
    mask = k_pos[None, :] <= q_pos[:, None]
    p = jax.nn.softmax(jnp.where(mask, s, -jnp.inf), axis=-1).astype(v.dtype)
    return jnp.einsum('bhtl,blhd->bthd', p, v)


def setup_inputs(seed: int = 0) -> dict:
    key = jax.random.key(seed)
    keys = jax.random.split(key, 64)
    ks = iter([keys[i] for i in range(64)])
    f32 = jnp.float32

    def w(shape, fan_in):
        return jax.random.normal(next(ks), shape, f32) * (fan_in ** -0.5)

    def gain(shape):
        return 1.0 + 0.02 * jax.random.normal(next(ks), shape, f32)

    n_pages = PAST_LEN // PAGE_SIZE
    n_used = DEC_BATCH * n_pages
    n_pool = n_used + max(1, n_used // 4)
    page_table = jax.random.permutation(next(ks), n_pool)[:n_used].reshape(DEC_BATCH, n_pages).astype(jnp.int32)

    return {
        "x_prompt": jax.random.normal(next(ks), (BATCH, SEQ, D_MODEL), f32),
        "x_sample": jax.random.normal(next(ks), (DEC_BATCH, DEC_SEQ, D_MODEL), f32),
        "mem_prompt": jax.random.normal(next(ks), (BATCH, MEM_TOKENS, D_MODEL), f32),
        "cache_mem_k": jax.random.normal(next(ks), (DEPTH, DEC_BATCH, MEM_TOKENS, MEM_HEADS, MEM_DH), f32),
        "cache_mem_v": jax.random.normal(next(ks), (DEPTH, DEC_BATCH, MEM_TOKENS, MEM_HEADS, MEM_DH), f32),
        "state_gla": jax.random.normal(next(ks), (N_A_LAYERS, DEC_BATCH, GLA_HEADS, GLA_DK, GLA_DV), f32),
        "cache_ckv": jax.random.normal(next(ks), (n_pool, PAGE_SIZE, MLA_KV_RANK), f32),
        "cache_kpe": jax.random.normal(next(ks), (n_pool, PAGE_SIZE, MLA_DR), f32),
        "page_table": page_table,
        "ffn1_norm": gain((DEPTH, D_MODEL)),
        "ffn1_w_gate": w((DEPTH, D_MODEL, D_FF), D_MODEL),
        "ffn1_w_up": w((DEPTH, D_MODEL, D_FF), D_MODEL),
        "ffn1_w_down": w((DEPTH, D_FF, D_MODEL), D_FF),
        "ffn2_norm": gain((DEPTH, D_MODEL)),
        "ffn2_w_gate": w((DEPTH, D_MODEL, D_FF), D_MODEL),
        "ffn2_w_up": w((DEPTH, D_MODEL, D_FF), D_MODEL),
        "ffn2_w_down": w((DEPTH, D_FF, D_MODEL), D_FF),
        "mix_norm": gain((DEPTH, D_MODEL)),
        "w_out": w((DEPTH, MIX_OUT, D_MODEL), MIX_OUT),
        "mem_norm": gain((DEPTH, D_MODEL)),
        "w_mem_k": w((DEPTH, D_MODEL, MEM_W), D_MODEL),
        "w_mem_v": w((DEPTH, D_MODEL, MEM_W), D_MODEL),
        "mem_k_norm": gain((DEPTH, MEM_DH)),
        "mem_q_norm": gain((DEPTH, MEM_DH)),
        "gla_w_in": w((N_A_LAYERS, D_MODEL, A_IN), D_MODEL),
        "gla_w_gate2": w((N_A_LAYERS, GLA_GATE_RANK, GLA_QK), GLA_GATE_RANK),
        "gla_b_gate": 0.1 * jax.random.normal(next(ks), (N_A_LAYERS, GLA_QK), f32),
        "gla_o_norm": gain((N_A_LAYERS, GLA_DV)),
        "mla_w_in": w((N_B_LAYERS, D_MODEL, B_IN), D_MODEL),
        "mla_q_a_norm": gain((N_B_LAYERS, MLA_Q_RANK)),
        "mla_w_uq": w((N_B_LAYERS, MLA_Q_RANK, MLA_HEADS * (MLA_DN + MLA_DR)), MLA_Q_RANK),
        "mla_q_nope_norm": gain((N_B_LAYERS, MLA_DN)),
        "mla_q_pe_norm": gain((N_B_LAYERS, MLA_DR)),
        "kv_norm": gain((D_MODEL,)),
        "kv_w_dkv": w((D_MODEL, MLA_KV_RANK), D_MODEL),
        "kv_ckv_norm": gain((MLA_KV_RANK,)),
        "kv_w_kr": w((D_MODEL, MLA_DR), D_MODEL),
        "kv_kpe_norm": gain((MLA_DR,)),
        "kv_w_uk": w((MLA_KV_RANK, MLA_HEADS * MLA_DN), MLA_KV_RANK),
        "kv_k_nope_norm": gain((MLA_DN,)),
        "kv_w_uv": w((MLA_KV_RANK, MLA_HEADS * MLA_DV), MLA_KV_RANK),
    }


def reference(x_prompt, x_sample, mem_prompt, cache_mem_k, cache_mem_v, state_gla, cache_ckv, cache_kpe, page_table,
              ffn1_norm, ffn1_w_gate, ffn1_w_up, ffn1_w_down, ffn2_norm, ffn2_w_gate, ffn2_w_up, ffn2_w_down,
              mix_norm, w_out, mem_norm, w_mem_k, w_mem_v, mem_k_norm, mem_q_norm,
              gla_w_in, gla_w_gate2, gla_b_gate, gla_o_norm,
              mla_w_in, mla_q_a_norm, mla_w_uq, mla_q_nope_norm, mla_q_pe_norm,
              kv_norm, kv_w_dkv, kv_ckv_norm, kv_w_kr, kv_kpe_norm, kv_w_uk, kv_k_nope_norm, kv_w_uv):
    a_splits = [GLA_QK, 2 * GLA_QK, 2 * GLA_QK + GLA_V, 2 * GLA_QK + GLA_V + GLA_GATE_RANK,
                2 * GLA_QK + 2 * GLA_V + GLA_GATE_RANK]

    def shared_latent(h, pos):
        hn = rms_norm(h, kv_norm)
        c = rms_norm(hn @ kv_w_dkv, kv_ckv_norm)
        kpe = rope(rms_norm(hn @ kv_w_kr, kv_kpe_norm), pos)
        return c, kpe

    def trunk(x, pos, mem_ks, mem_vs, gla_s0, make_attend):
        bsz, t, _ = x.shape
        gla_states = []
        c = kpe = None
        attend = None
        for l in range(DEPTH):
            if l == N_A_LAYERS:
                c, kpe = shared_latent(x, pos)
                attend = make_attend(c, kpe)
            x = swiglu_half(x, ffn1_norm[l], ffn1_w_gate[l], ffn1_w_up[l], ffn1_w_down[l])
            u = rms_norm(x, mix_norm[l])
            if l < N_A_LAYERS:
                p = u @ gla_w_in[l]
                q, k, v, glr, r, mq = jnp.split(p, a_splits, axis=-1)
                q = q.reshape(bsz, t, GLA_HEADS, GLA_DK) * (GLA_DK ** -0.5)
                log_a = jax.nn.log_sigmoid((glr @ gla_w_gate2[l] + gla_b_gate[l]).astype(jnp.float32)) / GLA_GATE_TEMP
                o, s_new = gla_recurrence(q, k.reshape(bsz, t, GLA_HEADS, GLA_DK),
                                          v.reshape(bsz, t, GLA_HEADS, GLA_DV),
                                          log_a.reshape(bsz, t, GLA_HEADS, GLA_DK), gla_s0[l])
                o_main = (rms_norm(o, gla_o_norm[l]).astype(x.dtype)
                          * jax.nn.silu(r).reshape(bsz, t, GLA_HEADS, GLA_DV)).reshape(bsz, t, GLA_V)
                gla_states.append(s_new.astype(x.dtype))
            else:
                j = l - N_A_LAYERS
                p = u @ mla_w_in[j]
                cq, mq = p[..., :MLA_Q_RANK], p[..., MLA_Q_RANK:]
                q = (rms_norm(cq, mla_q_a_norm[j]) @ mla_w_uq[j]).reshape(bsz, t, MLA_HEADS, MLA_DN + MLA_DR)
                qn = rms_norm(q[..., :MLA_DN], mla_q_nope_norm[j])
                qp = rope(rms_norm(q[..., MLA_DN:], mla_q_pe_norm[j]), pos)
                o_main = attend(qn, qp).reshape(bsz, t, MLA_V)
            mq = rms_norm(mq.reshape(bsz, t, MEM_HEADS, MEM_DH), mem_q_norm[l])
            o_mem = mem_attend(mq, mem_ks[l], mem_vs[l])
            x = x + jnp.concatenate([o_main, o_mem], axis=-1) @ w_out[l]
            x = swiglu_half(x, ffn2_norm[l], ffn2_w_gate[l], ffn2_w_up[l], ffn2_w_down[l])
        return x, jnp.stack(gla_states), c, kpe

    seq = x_prompt.shape[1]
    pos_p = jnp.arange(seq)
    mem_kv_p = [mem_kv(mem_prompt, mem_norm[l], w_mem_k[l], w_mem_v[l], mem_k_norm[l]) for l in range(DEPTH)]
    mem_k_p = jnp.stack([kv[0] for kv in mem_kv_p])
    mem_v_p = jnp.stack([kv[1] for kv in mem_kv_p])
    s0_p = jnp.zeros((N_A_LAYERS, x_prompt.shape[0], GLA_HEADS, GLA_DK, GLA_DV), x_prompt.dtype)

    def make_attend_prompt(c, kpe):
        kn, v = mla_kv(c, kv_w_uk, kv_k_nope_norm, kv_w_uv)

        def attend(qn, qp):
            bsz, t = qn.shape[:2]
            nb = t // Q_BLOCK

            def blk(a):
                return a.reshape((bsz, nb, Q_BLOCK) + a.shape[2:]).swapaxes(0, 1)

            def one(args):
                qn_b, qp_b, qpos_b = args
                return mla_attend(qn_b, qp_b, kn, kpe, v, qpos_b, pos_p)

            o = lax.map(one, (blk(qn), blk(qp), pos_p.reshape(nb, Q_BLOCK)))
            return o.swapaxes(0, 1).reshape(bsz, t, MLA_HEADS, MLA_DV)
        return attend

    y_prompt, gla_state_prompt, ckv_prompt, kpe_prompt = trunk(
        x_prompt, pos_p, mem_k_p, mem_v_p, s0_p, make_attend_prompt)

    past = page_table.shape[1] * cache_ckv.shape[1]
    t_dec = x_sample.shape[1]
    pos_s = past + jnp.arange(t_dec)
    k_pos_s = jnp.arange(past + t_dec)

    def make_attend_sample(c_new, kpe_new):
        def attend(qn, qp):
            def one(args):
                pt, qn1, qp1, c1, k1 = args
                c_all = jnp.concatenate([cache_ckv[pt].reshape(past, MLA_KV_RANK), c1], axis=0)[None]
                kpe_all = jnp.concatenate([cache_kpe[pt].reshape(past, MLA_DR), k1], axis=0)[None]
                kn, v = mla_kv(c_all, kv_w_uk, kv_k_nope_norm, kv_w_uv)
                return mla_attend(qn1[None], qp1[None], kn, kpe_all, v, pos_s, k_pos_s)[0]
            return lax.map(one, (page_table, qn, qp, c_new, kpe_new))
        return attend

    y_sample, gla_state_sample, ckv_sample, kpe_sample = trunk(
        x_sample, pos_s, cache_mem_k, cache_mem_v, state_gla, make_attend_sample)

    return (y_prompt, y_sample, gla_state_prompt, gla_state_sample, ckv_prompt, kpe_prompt,
            ckv_sample, kpe_sample, mem_k_p, mem_v_p)
```

```python
import functools
import math

import jax
import jax.numpy as jnp
from jax import lax
from jax.experimental import pallas as pl
from jax.experimental.pallas import tpu as pltpu

F32 = jnp.float32
BF16 = jnp.bfloat16

D_MODEL = 1024
D_FF = 2816
GLA_HEADS = 4
GLA_DK = 96
GLA_DV = 192
GLA_QK = GLA_HEADS * GLA_DK
GLA_V = GLA_HEADS * GLA_DV
GLA_GATE_RANK = 16
GLA_GATE_TEMP = 16.0
MEM_HEADS = 4
MEM_DH = 64
MEM_W = MEM_HEADS * MEM_DH
MLA_HEADS = 12
MLA_DN = 64
MLA_DR = 32
MLA_DV = 64
MLA_KV_RANK = 256
MLA_Q_RANK = 384
ROPE_THETA = 10000.0
EPS = 1e-6

LANE = 128
DK_PAD = LANE
DV_PAD = 2 * LANE
HEAD_PAD = LANE
GLA_CHUNK = 64
T_PAD = 8
NEG = -1e30
VMEM_LIMIT = 56 * 1024 * 1024
FFN_ROWS = 1024
PROJ_ROWS = 512
A_IN_ROWS = 256

NT_DIMS = (((1,), (1,)), ((), ()))
TN_DIMS = (((0,), (0,)), ((), ()))


def _dot(a, b):
    return jnp.dot(a, b, preferred_element_type=F32)


def _dot_nt(a, b):
    return lax.dot_general(a, b, NT_DIMS, preferred_element_type=F32)


def _dot_tn(a, b):
    return lax.dot_general(a, b, TN_DIMS, preferred_element_type=F32)


def _rms(x, g):
    return x * lax.rsqrt(jnp.mean(x * x, axis=-1, keepdims=True) + EPS) * g


def _split3(x):
    hi = x.astype(BF16)
    r1 = x - hi.astype(F32)
    mid = r1.astype(BF16)
    lo = (r1 - mid.astype(F32)).astype(BF16)
    return hi, mid, lo


def _group_sum(x, ones_bd):
    hi, mid, _ = _split3(x)
    return _dot(hi, ones_bd) + _dot(mid, ones_bd)


def _params(sem):
    return pltpu.CompilerParams(dimension_semantics=sem, vmem_limit_bytes=VMEM_LIMIT)


def _ffn_kernel(x_ref, g_ref, wg_ref, wu_ref, wd_ref, o_ref, h_sc, acc_sc):
    f = pl.program_id(1)

    @pl.when(f == 0)
    def _():
        h_sc[...] = _rms(x_ref[...], g_ref[...]).astype(BF16)
        acc_sc[...] = jnp.zeros_like(acc_sc)

    h = h_sc[...]
    gate = _dot(h, wg_ref[...])
    up = _dot(h, wu_ref[...])
    act = (gate * jax.nn.sigmoid(gate) * up).astype(BF16)
    acc_sc[...] += _dot(act, wd_ref[...])

    @pl.when(f == pl.num_programs(1) - 1)
    def _():
        o_ref[...] = x_ref[...] + 0.5 * acc_sc[...]


def _ffn_half(x, g, wg, wu, wd, tm):
    n = x.shape[0]
    tf = 256
    return pl.pallas_call(
        _ffn_kernel,
        out_shape=jax.ShapeDtypeStruct((n, D_MODEL), F32),
        grid=(n // tm, D_FF // tf),
        in_specs=[
            pl.BlockSpec((tm, D_MODEL), lambda i, f: (i, 0)),
            pl.BlockSpec((1, D_MODEL), lambda i, f: (0, 0)),
            pl.BlockSpec((D_MODEL, tf), lambda i, f: (0, f)),
            pl.BlockSpec((D_MODEL, tf), lambda i, f: (0, f)),
            pl.BlockSpec((tf, D_MODEL), lambda i, f: (f, 0)),
        ],
        out_specs=pl.BlockSpec((tm, D_MODEL), lambda i, f: (i, 0)),
        scratch_shapes=[pltpu.VMEM((tm, D_MODEL), BF16), pltpu.VMEM((tm, D_MODEL), F32)],
        compiler_params=_params(("parallel", "arbitrary")),
        name="ffn_half",
    )(x, g, wg, wu, wd)


A_Q0 = 0
A_K0 = A_Q0 + GLA_HEADS * DK_PAD
A_V0 = A_K0 + GLA_HEADS * DK_PAD
A_R0 = A_V0 + GLA_HEADS * DV_PAD
A_M0 = A_R0 + GLA_HEADS * DV_PAD
A_G0 = A_M0 + MEM_W
A_COLS = A_G0 + LANE


def _head_rms(x, ones_bd, size, g):
    return x * lax.rsqrt(_group_sum(x * x, ones_bd) * (1.0 / size) + EPS) * g


def _a_in_kernel(x_ref, g_ref, w_ref, w2_ref, b2_ref, gq_ref, ones_ref,
                 q_ref, k_ref, v_ref, la_ref, r_ref, mq_ref):
    u = _rms(x_ref[...], g_ref[...]).astype(BF16)
    p = _dot(u, w_ref[...])
    q_ref[...] = p[:, A_Q0:A_K0] * (GLA_DK ** -0.5)
    k_ref[...] = p[:, A_K0:A_V0]
    v_ref[...] = p[:, A_V0:A_R0]
    r = p[:, A_R0:A_M0]
    r_ref[...] = r * jax.nn.sigmoid(r)
    mq_ref[...] = _head_rms(p[:, A_M0:A_G0], ones_ref[...], MEM_DH, gq_ref[...])
    z = _dot(p[:, A_G0:A_COLS].astype(BF16), w2_ref[...]) + b2_ref[...]
    log_sig = jnp.minimum(z, 0.0) - jnp.log1p(jnp.exp(-jnp.abs(z)))
    la_ref[...] = log_sig * (1.0 / GLA_GATE_TEMP)


def _a_in(x, g, w, w2, b2, gq, ones64, tm):
    n = x.shape[0]
    qk = GLA_HEADS * DK_PAD
    vv = GLA_HEADS * DV_PAD
    row = lambda c: pl.BlockSpec((tm, c), lambda i: (i, 0))
    full = lambda a: pl.BlockSpec(a.shape, lambda i: (0,) * a.ndim)
    return pl.pallas_call(
        _a_in_kernel,
        out_shape=(jax.ShapeDtypeStruct((n, qk), F32), jax.ShapeDtypeStruct((n, qk), F32),
                   jax.ShapeDtypeStruct((n, vv), F32), jax.ShapeDtypeStruct((n, qk), F32),
                   jax.ShapeDtypeStruct((n, vv), F32), jax.ShapeDtypeStruct((n, MEM_W), F32)),
        grid=(n // tm,),
        in_specs=[row(D_MODEL), full(g), full(w), full(w2), full(b2), full(gq), full(ones64)],
        out_specs=(row(qk), row(qk), row(vv), row(qk), row(vv), row(MEM_W)),
        compiler_params=_params(("parallel",)),
        name="layer_a_in",
    )(x, g, w, w2, b2, gq, ones64)


def _gla_kernel(q_ref, k_ref, v_ref, la_ref, s0_ref, o_ref, s_out_ref, st_sc):
    c = pl.program_id(1)
    chunk = q_ref.shape[1]

    @pl.when(c == 0)
    def _():
        st_sc[...] = s0_ref[0]

    la = la_ref[0]
    t_idx = lax.broadcasted_iota(jnp.int32, (chunk, chunk), 0)
    s_idx = lax.broadcasted_iota(jnp.int32, (chunk, chunk), 1)
    causal = s_idx <= t_idx
    tri = jnp.where(causal, 1.0, 0.0).astype(BF16)
    hi, mid, lo = _split3(la)
    a_cum = _dot(tri, hi) + _dot(tri, mid) + _dot(tri, lo)
    a_last = a_cum[chunk - 1:chunk, :]
    e_q = jnp.exp(a_cum)
    e_k = jnp.exp(-a_cum)
    e_kd = jnp.exp(a_last - a_cum)
    e_last = jnp.exp(a_last)
    q = q_ref[0]
    k = k_ref[0]
    v = v_ref[0]
    for h in range(GLA_HEADS):
        ks = slice(h * DK_PAD, (h + 1) * DK_PAD)
        vs = slice(h * DV_PAD, (h + 1) * DV_PAD)
        qh = (q[:, ks] * e_q[:, ks]).astype(BF16)
        kh = (k[:, ks] * e_k[:, ks]).astype(BF16)
        kd = (k[:, ks] * e_kd[:, ks]).astype(BF16)
        vh = v[:, vs].astype(BF16)
        st = st_sc[h]
        scores = jnp.where(causal, _dot_nt(qh, kh), 0.0).astype(BF16)
        o_ref[0, :, vs] = _dot(scores, vh) + _dot_nt(qh, st.astype(BF16))
        st_sc[h] = st * e_last[:, ks] + _dot_tn(vh, kd)

    @pl.when(c == pl.num_programs(1) - 1)
    def _():
        s_out_ref[0] = st_sc[...]


def _gla(q, k, v, la, s0t, chunk):
    b, t, _ = q.shape
    qk = GLA_HEADS * DK_PAD
    vv = GLA_HEADS * DV_PAD
    seq = lambda c: pl.BlockSpec((1, chunk, c), lambda i, j: (i, j, 0))
    st = pl.BlockSpec((1, GLA_HEADS, DV_PAD, DK_PAD), lambda i, j: (i, 0, 0, 0))
    return pl.pallas_call(
        _gla_kernel,
        out_shape=(jax.ShapeDtypeStruct((b, t, vv), F32),
                   jax.ShapeDtypeStruct((b, GLA_HEADS, DV_PAD, DK_PAD), F32)),
        grid=(b, t // chunk),
        in_specs=[seq(qk), seq(qk), seq(vv), seq(qk), st],
        out_specs=(seq(vv), st),
        scratch_shapes=[pltpu.VMEM((GLA_HEADS, DV_PAD, DK_PAD), F32)],
        compiler_params=_params(("parallel", "arbitrary")),
        name="gla_recurrence",
    )(q, k, v, la, s0t)


def _mem_attn_kernel(q_ref, k_ref, v_ref, o_ref):
    q = q_ref[0]
    k = k_ref[0].astype(BF16)
    v = v_ref[0].astype(BF16)
    head = lax.broadcasted_iota(jnp.int32, (1, MEM_W), 1) // MEM_DH
    acc = jnp.zeros(q.shape, F32)
    for h in range(MEM_HEADS):
        sel = head == h
        qh = jnp.where(sel, q, 0.0).astype(BF16)
        s = _dot_nt(qh, k) * (MEM_DH ** -0.5)
        p = jnp.exp(s - jnp.max(s, axis=-1, keepdims=True))
        pv = _dot(p.astype(BF16), v) / jnp.sum(p, axis=-1, keepdims=True)
        acc = acc + jnp.where(sel, pv, 0.0)
    o_ref[0] = acc


def _mem_attn(q, mk, mv, tq):
    b, t, _ = q.shape
    m = mk.shape[1]
    return pl.pallas_call(
        _mem_attn_kernel,
        out_shape=jax.ShapeDtypeStruct((b, t, MEM_W), F32),
        grid=(b, t // tq),
        in_specs=[pl.BlockSpec((1, tq, MEM_W), lambda i, j: (i, j, 0)),
                  pl.BlockSpec((1, m, MEM_W), lambda i, j: (i, 0, 0)),
                  pl.BlockSpec((1, m, MEM_W), lambda i, j: (i, 0, 0))],
        out_specs=pl.BlockSpec((1, tq, MEM_W), lambda i, j: (i, j, 0)),
        compiler_params=_params(("parallel", "parallel")),
        name="mem_attention",
    )(q, mk, mv)


def _a_out_kernel(x_ref, o_ref, r_ref, go_ref, om_ref, wa_ref, wm_ref, y_ref):
    o = o_ref[...]
    parts = []
    for h in range(GLA_HEADS):
        oh = o[:, h * DV_PAD:(h + 1) * DV_PAD]
        ms = jnp.sum(oh * oh, axis=-1, keepdims=True) * (1.0 / GLA_DV)
        parts.append(oh * lax.rsqrt(ms + EPS))
    on = jnp.concatenate(parts, axis=-1) * go_ref[...]
    main = (on * r_ref[...]).astype(BF16)
    y_ref[...] = (x_ref[...] + _dot(main, wa_ref[...])
                  + _dot(om_ref[...].astype(BF16), wm_ref[...]))


def _a_out(x, o, r, go, om, wa, wm, tm):
    n = x.shape[0]
    row = lambda c: pl.BlockSpec((tm, c), lambda i: (i, 0))
    full = lambda a: pl.BlockSpec(a.shape, lambda i: (0,) * a.ndim)
    return pl.pallas_call(
        _a_out_kernel,
        out_shape=jax.ShapeDtypeStruct((n, D_MODEL), F32),
        grid=(n // tm,),
        in_specs=[row(D_MODEL), row(o.shape[1]), row(r.shape[1]), full(go), row(MEM_W),
                  full(wa), full(wm)],
        out_specs=row(D_MODEL),
        compiler_params=_params(("parallel",)),
        name="layer_a_out",
    )(x, o, r, go, om, wa, wm)


def _b_out_kernel(x_ref, o_ref, om_ref, wa_ref, wm_ref, y_ref):
    y_ref[...] = (x_ref[...] + _dot(o_ref[...].astype(BF16), wa_ref[...])
                  + _dot(om_ref[...].astype(BF16), wm_ref[...]))


def _b_out(x, o, om, wa, wm, tm):
    n = x.shape[0]
    row = lambda c: pl.BlockSpec((tm, c), lambda i: (i, 0))
    full = lambda a: pl.BlockSpec(a.shape, lambda i: (0,) * a.ndim)
    return pl.pallas_call(
        _b_out_kernel,
        out_shape=jax.ShapeDtypeStruct((n, D_MODEL), F32),
        grid=(n // tm,),
        in_specs=[row(D_MODEL), row(o.shape[1]), row(MEM_W), full(wa), full(wm)],
        out_specs=row(D_MODEL),
        compiler_params=_params(("parallel",)),
        name="layer_b_out",
    )(x, o, om, wa, wm)


L_C0 = 0
L_A0 = MLA_KV_RANK
L_B0 = L_A0 + LANE
L_COLS = L_B0 + LANE
KV_W = MLA_HEADS * HEAD_PAD


def _latent_kernel(x_ref, g_ref, wl_ref, gc_ref, ga_ref, gb_ref, cos_ref, sin_ref,
                   wuk_ref, gkn_ref, wuv_ref, ones_ref,
                   c_ref, kpe_ref, kf_ref, vf_ref):
    hn = _rms(x_ref[...], g_ref[...]).astype(BF16)
    y = _dot(hn, wl_ref[...])
    c = _rms(y[:, L_C0:L_A0], gc_ref[...])
    c_ref[...] = c
    a = y[:, L_A0:L_B0]
    b = y[:, L_B0:L_COLS]
    r = lax.rsqrt(jnp.sum(a * a, axis=-1, keepdims=True) * (1.0 / MLA_DR) + EPS)
    kpe = a * r * ga_ref[...] * cos_ref[...] + b * r * gb_ref[...] * sin_ref[...]
    kpe_ref[...] = kpe
    cb = c.astype(BF16)
    kn = _dot(cb, wuk_ref[...])
    ones = ones_ref[...]
    gkn = gkn_ref[...]
    for h in range(MLA_HEADS):
        hs = slice(h * HEAD_PAD, (h + 1) * HEAD_PAD)
        knh = kn[:, hs]
        ms = _group_sum(knh * knh, ones) * (1.0 / MLA_DN)
        kf_ref[:, hs] = (knh * lax.rsqrt(ms + EPS) * gkn + kpe).astype(BF16)
    vf_ref[...] = _dot(cb, wuv_ref[...]).astype(BF16)


def _latent(x, g, wl, gc, ga, gb, cos, sin, wuk, gkn, wuv, ones_h, tm):
    n = x.shape[0]
    row = lambda c: pl.BlockSpec((tm, c), lambda i: (i, 0))
    full = lambda a: pl.BlockSpec(a.shape, lambda i: (0,) * a.ndim)
    return pl.pallas_call(
        _latent_kernel,
        out_shape=(jax.ShapeDtypeStruct((n, MLA_KV_RANK), F32), jax.ShapeDtypeStruct((n, LANE), F32),
                   jax.ShapeDtypeStruct((n, KV_W), BF16), jax.ShapeDtypeStruct((n, KV_W), BF16)),
        grid=(n // tm,),
        in_specs=[row(D_MODEL), full(g), full(wl), full(gc), full(ga), full(gb), row(LANE), row(LANE),
                  full(wuk), full(gkn), full(wuv), full(ones_h)],
        out_specs=(row(MLA_KV_RANK), row(LANE), row(KV_W), row(KV_W)),
        compiler_params=_params(("parallel",)),
        name="shared_latent",
    )(x, g, wl, gc, ga, gb, cos, sin, wuk, gkn, wuv, ones_h)


B_COLS = MLA_Q_RANK + MEM_W


def _b_in_kernel(x_ref, g_ref, w_ref, gqa_ref, w1_ref, w2_ref, g1_ref, g2_ref, inv_ref,
                 cos_ref, sin_ref, gq_ref, onesh_ref, ones64_ref, q_ref, mq_ref):
    u = _rms(x_ref[...], g_ref[...]).astype(BF16)
    p = _dot(u, w_ref[...])
    cq = _rms(p[:, :MLA_Q_RANK], gqa_ref[...]).astype(BF16)
    q1 = _dot(cq, w1_ref[...])
    q2 = _dot(cq, w2_ref[...])
    onesh = onesh_ref[...]
    inv = inv_ref[...]
    c1 = g1_ref[...] * cos_ref[...]
    c2 = g2_ref[...] * sin_ref[...]
    for h in range(MLA_HEADS):
        hs = slice(h * HEAD_PAD, (h + 1) * HEAD_PAD)
        q1h = q1[:, hs]
        rs = lax.rsqrt(_group_sum(q1h * q1h, onesh) * inv + EPS)
        q_ref[:, hs] = ((q1h * c1 + q2[:, hs] * c2) * rs).astype(BF16)
    mq_ref[...] = _head_rms(p[:, MLA_Q_RANK:B_COLS], ones64_ref[...], MEM_DH, gq_ref[...])


def _b_in(x, g, w, gqa, w1, w2, g1, g2, inv, cos, sin, gq, ones_h, ones64, tm):
    n = x.shape[0]
    row = lambda c: pl.BlockSpec((tm, c), lambda i: (i, 0))
    full = lambda a: pl.BlockSpec(a.shape, lambda i: (0,) * a.ndim)
    return pl.pallas_call(
        _b_in_kernel,
        out_shape=(jax.ShapeDtypeStruct((n, KV_W), BF16), jax.ShapeDtypeStruct((n, MEM_W), F32)),
        grid=(n // tm,),
        in_specs=[row(D_MODEL), full(g), full(w), full(gqa), full(w1), full(w2), full(g1), full(g2),
                  full(inv), row(LANE), row(LANE), full(gq), full(ones_h), full(ones64)],
        out_specs=(row(KV_W), row(MEM_W)),
        compiler_params=_params(("parallel",)),
        name="layer_b_in",
    )(x, g, w, gqa, w1, w2, g1, g2, inv, cos, sin, gq, ones_h, ones64)


def _flash_kernel(qi_ref, ki_ref, q_ref, k_ref, v_ref, o_ref, m_sc, l_sc, acc_sc):
    s_idx = pl.program_id(2)
    qi = qi_ref[s_idx]
    ki = ki_ref[s_idx]
    tq = q_ref.shape[1]
    tk = k_ref.shape[1]

    @pl.when(ki == 0)
    def _():
        m_sc[...] = jnp.full_like(m_sc, NEG)
        l_sc[...] = jnp.zeros_like(l_sc)
        acc_sc[...] = jnp.zeros_like(acc_sc)

    s = _dot_nt(q_ref[0], k_ref[0])
    row = qi * tq + lax.broadcasted_iota(jnp.int32, (tq, tk), 0)
    col = ki * tk + lax.broadcasted_iota(jnp.int32, (tq, tk), 1)
    s = jnp.where(col <= row, s, NEG)
    m_prev = m_sc[...]
    m_new = jnp.maximum(m_prev, jnp.max(s, axis=-1, keepdims=True))
    alpha = jnp.exp(m_prev - m_new)
    p = jnp.exp(s - m_new)
    l_sc[...] = alpha * l_sc[...] + jnp.sum(p, axis=-1, keepdims=True)
    acc_sc[...] = alpha * acc_sc[...] + _dot(p.astype(BF16), v_ref[0])
    m_sc[...] = m_new

    @pl.when(ki == qi)
    def _():
        o_ref[0] = (acc_sc[...] / l_sc[...]).astype(o_ref.dtype)


def _flash(q, k, v, tq):
    b, t, _ = q.shape
    nq = t // tq
    pairs = [(i, j) for i in range(nq) for j in range(i + 1)]
    qi = jnp.asarray([p[0] for p in pairs], jnp.int32)
    ki = jnp.asarray([p[1] for p in pairs], jnp.int32)
    qspec = pl.BlockSpec((1, tq, HEAD_PAD), lambda bi, h, s, qi, ki: (bi, qi[s], h))
    kspec = pl.BlockSpec((1, tq, HEAD_PAD), lambda bi, h, s, qi, ki: (bi, ki[s], h))
    return pl.pallas_call(
        _flash_kernel,
        out_shape=jax.ShapeDtypeStruct((b, t, KV_W), BF16),
        grid_spec=pltpu.PrefetchScalarGridSpec(
            num_scalar_prefetch=2,
            grid=(b, MLA_HEADS, len(pairs)),
            in_specs=[qspec, kspec, kspec],
            out_specs=qspec,
            scratch_shapes=[pltpu.VMEM((tq, 1), F32), pltpu.VMEM((tq, 1), F32),
                            pltpu.VMEM((tq, HEAD_PAD), F32)],
        ),
        compiler_params=_params(("parallel", "parallel", "arbitrary")),
        name="prompt_latent_attention",
    )(qi, ki, q, k, v)


PAGES_PER_STEP = 16
QROWS = MLA_HEADS * T_PAD
SUB = 256


def _absorb_kernel(q_ref, w_ref, o_ref):
    o_ref[0] = _dot(q_ref[...], w_ref[0]).astype(BF16)


def _absorb(q, wabs):
    n = q.shape[0]
    return pl.pallas_call(
        _absorb_kernel,
        out_shape=jax.ShapeDtypeStruct((MLA_HEADS, n, MLA_KV_RANK), BF16),
        grid=(MLA_HEADS,),
        in_specs=[pl.BlockSpec((n, HEAD_PAD), lambda h: (0, h)),
                  pl.BlockSpec((1, HEAD_PAD, MLA_KV_RANK), lambda h: (h, 0, 0))],
        out_specs=pl.BlockSpec((1, n, MLA_KV_RANK), lambda h: (h, 0, 0)),
        compiler_params=_params(("parallel",)),
        name="absorb_w_uk",
    )(q, wabs)


def _scores_block(qa, qp, wukt, cb, kb):
    kt = _dot_nt(wukt, cb)
    sn = _dot_nt(qa, cb)
    sp = _dot_nt(qp, kb)
    rows = []
    for h in range(MLA_HEADS):
        kth = kt[h * MLA_DN:(h + 1) * MLA_DN, :]
        r = lax.rsqrt(jnp.sum(kth * kth, axis=0, keepdims=True) * (1.0 / MLA_DN) + EPS)
        rows.append(sn[h * T_PAD:(h + 1) * T_PAD, :] * r)
    return jnp.concatenate(rows, axis=0) + sp


def _sample_attn_kernel(pt_ref, *refs):
    npg = (len(refs) - 12) // 2
    c_refs = refs[:npg]
    k_refs = refs[npg:2 * npg]
    qa_ref, qp_ref, wukt_ref, cn_ref, kn_ref, o_ref, cb_sc, kb_sc, s_sc, m_sc, l_sc, acc_sc = refs[2 * npg:]
    j = pl.program_id(1)
    page = c_refs[0].shape[1]

    @pl.when(j == 0)
    def _():
        m_sc[...] = jnp.full_like(m_sc, NEG)
        l_sc[...] = jnp.zeros_like(l_sc)
        acc_sc[...] = jnp.zeros_like(acc_sc)

    qa = qa_ref[0]
    qp = qp_ref[0]
    wukt = wukt_ref[...]

    def update(s, cb):
        m_prev = m_sc[...]
        m_new = jnp.maximum(m_prev, jnp.max(s, axis=-1, keepdims=True))
        alpha = jnp.exp(m_prev - m_new)
        p = jnp.exp(s - m_new)
        l_sc[...] = alpha * l_sc[...] + jnp.sum(p, axis=-1, keepdims=True)
        acc_sc[...] = alpha * acc_sc[...] + _dot(p.astype(BF16), cb)
        m_sc[...] = m_new

    for i in range(npg):
        cb_sc[i * page:(i + 1) * page, :] = c_refs[i][0].astype(BF16)
        kb_sc[i * page:(i + 1) * page, :] = k_refs[i][0].astype(BF16)
    for i in range(npg * page // SUB):
        sl = slice(i * SUB, (i + 1) * SUB)
        s_sc[:, sl] = _scores_block(qa, qp, wukt, cb_sc[sl, :], kb_sc[sl, :])
    update(s_sc[...], cb_sc[...])

    @pl.when(j == pl.num_programs(1) - 1)
    def _():
        cn = cn_ref[0].astype(BF16)
        s = _scores_block(qa, qp, wukt, cn, kn_ref[0].astype(BF16))
        t_q = lax.broadcasted_iota(jnp.int32, s.shape, 0) % T_PAD
        t_k = lax.broadcasted_iota(jnp.int32, s.shape, 1)
        update(jnp.where(t_k <= t_q, s, NEG), cn)
        o_ref[0] = acc_sc[...] / l_sc[...]


def _sample_attn(page_table, cache_ckv, cache_kpe, qa, qp, wukt, c_new, k_new):
    nb, n_pages = page_table.shape
    page = cache_ckv.shape[1]
    npg = math.gcd(n_pages, PAGES_PER_STEP)
    steps = n_pages // npg

    def pspec(i, width):
        return pl.BlockSpec((1, page, width), lambda b, j, pt: (pt[b, j * npg + i], 0, 0))

    per_b = lambda r, w: pl.BlockSpec((1, r, w), lambda b, j, pt: (b, 0, 0))
    in_specs = ([pspec(i, MLA_KV_RANK) for i in range(npg)] + [pspec(i, MLA_DR) for i in range(npg)]
                + [per_b(QROWS, MLA_KV_RANK), per_b(QROWS, MLA_DR),
                   pl.BlockSpec(wukt.shape, lambda b, j, pt: (0, 0)),
                   per_b(c_new.shape[1], MLA_KV_RANK), per_b(k_new.shape[1], MLA_DR)])
    return pl.pallas_call(
        _sample_attn_kernel,
        out_shape=jax.ShapeDtypeStruct((nb, QROWS, MLA_KV_RANK), F32),
        grid_spec=pltpu.PrefetchScalarGridSpec(
            num_scalar_prefetch=1,
            grid=(nb, steps),
            in_specs=in_specs,
            out_specs=per_b(QROWS, MLA_KV_RANK),
            scratch_shapes=[pltpu.VMEM((npg * page, MLA_KV_RANK), BF16),
                            pltpu.VMEM((npg * page, MLA_DR), BF16),
                            pltpu.VMEM((QROWS, npg * page), F32),
                            pltpu.VMEM((QROWS, 1), F32), pltpu.VMEM((QROWS, 1), F32),
                            pltpu.VMEM((QROWS, MLA_KV_RANK), F32)],
        ),
        compiler_params=_params(("parallel", "arbitrary")),
        name="sample_latent_attention",
    )(page_table, *([cache_ckv] * npg), *([cache_kpe] * npg), qa, qp, wukt, c_new, k_new)


def _uv_kernel(x_ref, w_ref, o_ref):
    o_ref[0] = _dot(x_ref[0].astype(BF16), w_ref[0])


def _uv_project(ctx, wuv):
    _, n, _ = ctx.shape
    return pl.pallas_call(
        _uv_kernel,
        out_shape=jax.ShapeDtypeStruct((MLA_HEADS, n, HEAD_PAD), F32),
        grid=(MLA_HEADS,),
        in_specs=[pl.BlockSpec((1, n, MLA_KV_RANK), lambda h: (h, 0, 0)),
                  pl.BlockSpec((1, MLA_KV_RANK, HEAD_PAD), lambda h: (h, 0, 0))],
        out_specs=pl.BlockSpec((1, n, HEAD_PAD), lambda h: (h, 0, 0)),
        compiler_params=_params(("parallel",)),
        name="apply_w_uv",
    )(ctx, wuv)


def _memkv_kernel(m_ref, g_ref, wk_ref, wv_ref, gk_ref, ones_ref, k_ref, v_ref):
    mn = _rms(m_ref[...], g_ref[0]).astype(BF16)
    k_ref[0] = _head_rms(_dot(mn, wk_ref[0]), ones_ref[...], MEM_DH, gk_ref[0])
    v_ref[0] = _dot(mn, wv_ref[0])


def _memkv(mem, g, wk, wv, gk, ones64):
    n = mem.shape[0]
    nl = wk.shape[0]
    lay = lambda a: pl.BlockSpec((1,) + a.shape[1:], lambda l: (l,) + (0,) * (a.ndim - 1))
    out = pl.BlockSpec((1, n, MEM_W), lambda l: (l, 0, 0))
    return pl.pallas_call(
        _memkv_kernel,
        out_shape=(jax.ShapeDtypeStruct((nl, n, MEM_W), F32), jax.ShapeDtypeStruct((nl, n, MEM_W), F32)),
        grid=(nl,),
        in_specs=[pl.BlockSpec(mem.shape, lambda l: (0, 0)), lay(g), lay(wk), lay(wv), lay(gk),
                  pl.BlockSpec(ones64.shape, lambda l: (0, 0))],
        out_specs=(out, out),
        compiler_params=_params(("parallel",)),
        name="prompt_mem_kv",
    )(mem, g, wk, wv, gk, ones64)


def _pad_heads(w, heads, size, padded, axis=-1):
    axis = axis % w.ndim
    shape = w.shape[:axis] + (heads, size) + w.shape[axis + 1:]
    w = w.reshape(shape)
    pad = [(0, 0)] * w.ndim
    pad[axis + 1] = (0, padded - size)
    w = jnp.pad(w, pad)
    return w.reshape(w.shape[:axis] + (heads * padded,) + w.shape[axis + 2:])


def _block_ones(n, sizes):
    gid = []
    g = 0
    while len(gid) < n:
        for s in sizes:
            gid += [g] * s
            g += 1
    gid = jnp.asarray(gid[:n], jnp.int32)
    return (gid[:, None] == gid[None, :]).astype(BF16)


def _rope_tables(pos):
    half = MLA_DR // 2
    inv = ROPE_THETA ** (-jnp.arange(half, dtype=F32) / half)
    ang = pos.astype(F32)[:, None] * inv
    cos, sin = jnp.cos(ang), jnp.sin(ang)
    z = jnp.zeros((pos.shape[0], MLA_DN), F32)
    z2 = jnp.zeros((pos.shape[0], HEAD_PAD - MLA_DN - MLA_DR), F32)
    return (jnp.concatenate([z + 1.0, cos, cos, z2], axis=-1), jnp.concatenate([z, -sin, sin, z2], axis=-1))


def _swap_halves(w):
    half = w.shape[-1] // 2
    return jnp.concatenate([w[..., half:], w[..., :half]], axis=-1)


def _rope_slab(w):
    pad = [(0, 0)] * (w.ndim - 1) + [(MLA_DN, HEAD_PAD - MLA_DN - MLA_DR)]
    return jnp.pad(w, pad)


def _tile_for(n, pref):
    t = min(n, pref)
    while n % t:
        t //= 2
    return t


def kernel(x_prompt, x_sample, mem_prompt, cache_mem_k, cache_mem_v, state_gla, cache_ckv, cache_kpe, page_table,
           ffn1_norm, ffn1_w_gate, ffn1_w_up, ffn1_w_down, ffn2_norm, ffn2_w_gate, ffn2_w_up, ffn2_w_down,
           mix_norm, w_out, mem_norm, w_mem_k, w_mem_v, mem_k_norm, mem_q_norm,
           gla_w_in, gla_w_gate2, gla_b_gate, gla_o_norm,
           mla_w_in, mla_q_a_norm, mla_w_uq, mla_q_nope_norm, mla_q_pe_norm,
           kv_norm, kv_w_dkv, kv_ckv_norm, kv_w_kr, kv_kpe_norm, kv_w_uk, kv_k_nope_norm, kv_w_uv):
    bp, tp, _ = x_prompt.shape
    bs, ts, _ = x_sample.shape
    n_mem = mem_prompt.shape[1]
    past = page_table.shape[1] * cache_ckv.shape[1]
    row = lambda a: a.reshape(1, -1).astype(F32)
    bf = lambda a: a.astype(BF16)

    ffn = [[(row(n[l]), bf(g[l]), bf(u[l]), bf(d[l])) for l in range(2)]
           for n, g, u, d in ((ffn1_norm, ffn1_w_gate, ffn1_w_up, ffn1_w_down),
                              (ffn2_norm, ffn2_w_gate, ffn2_w_up, ffn2_w_down))]
    ones64 = _block_ones(MEM_W, [MEM_DH])
    ones_head = _block_ones(HEAD_PAD, [MLA_DN, MLA_DR, HEAD_PAD - MLA_DN - MLA_DR])
    gmq = [row(jnp.tile(mem_q_norm[l], MEM_HEADS)) for l in range(2)]

    wi = gla_w_in[0]
    o_q, o_k, o_v, o_g, o_r, o_m = 0, GLA_QK, 2 * GLA_QK, 2 * GLA_QK + GLA_V, 2 * GLA_QK + GLA_V + GLA_GATE_RANK, \
        2 * GLA_QK + 2 * GLA_V + GLA_GATE_RANK
    wa_in = bf(jnp.concatenate([
        _pad_heads(wi[:, o_q:o_k], GLA_HEADS, GLA_DK, DK_PAD),
        _pad_heads(wi[:, o_k:o_v], GLA_HEADS, GLA_DK, DK_PAD),
        _pad_heads(wi[:, o_v:o_g], GLA_HEADS, GLA_DV, DV_PAD),
        _pad_heads(wi[:, o_r:o_m], GLA_HEADS, GLA_DV, DV_PAD),
        wi[:, o_m:],
        jnp.pad(wi[:, o_g:o_r], ((0, 0), (0, LANE - GLA_GATE_RANK))),
    ], axis=1))
    w_gate2 = bf(jnp.pad(_pad_heads(gla_w_gate2[0], GLA_HEADS, GLA_DK, DK_PAD),
                         ((0, LANE - GLA_GATE_RANK), (0, 0))))
    b_gate = row(_pad_heads(gla_b_gate[0], GLA_HEADS, GLA_DK, DK_PAD))
    g_o = row(jnp.pad(gla_o_norm[0], (0, DV_PAD - GLA_DV)))
    g_o = jnp.tile(g_o, (1, GLA_HEADS))
    wa_out_main = bf(_pad_heads(w_out[0][:GLA_V], GLA_HEADS, GLA_DV, DV_PAD, axis=0))
    wa_out_mem = bf(w_out[0][GLA_V:])

    kr_slab = _rope_slab(kv_w_kr)
    w_lat = bf(jnp.concatenate([kv_w_dkv, kr_slab, _rope_slab(_swap_halves(kv_w_kr))], axis=1))
    g_kpe_a = row(_rope_slab(kv_kpe_norm))
    g_kpe_b = row(_rope_slab(_swap_halves(kv_kpe_norm)))
    w_uk_pad = bf(_pad_heads(kv_w_uk, MLA_HEADS, MLA_DN, HEAD_PAD))
    w_uv_pad = bf(_pad_heads(kv_w_uv, MLA_HEADS, MLA_DV, HEAD_PAD))
    g_kn_slab = row(jnp.pad(kv_k_nope_norm, (0, HEAD_PAD - MLA_DN)))

    scale = (MLA_DN + MLA_DR) ** -0.5
    wuq = mla_w_uq[0].reshape(MLA_Q_RANK, MLA_HEADS, MLA_DN + MLA_DR)
    w_q1 = bf(jnp.pad(wuq, ((0, 0), (0, 0), (0, HEAD_PAD - MLA_DN - MLA_DR))).reshape(MLA_Q_RANK, KV_W))
    w_q2 = bf(_rope_slab(_swap_halves(wuq[..., MLA_DN:])).reshape(MLA_Q_RANK, KV_W))
    g_q1 = row(jnp.concatenate([mla_q_nope_norm[0], mla_q_pe_norm[0],
                                jnp.zeros((HEAD_PAD - MLA_DN - MLA_DR,), F32)])) * scale
    g_q2 = row(_rope_slab(_swap_halves(mla_q_pe_norm[0]))) * scale
    inv_sizes = row(jnp.concatenate([jnp.full((MLA_DN,), 1.0 / MLA_DN, F32),
                                     jnp.full((HEAD_PAD - MLA_DN,), 1.0 / MLA_DR, F32)]))
    wb_in = bf(mla_w_in[0])
    wb_out_main = bf(_pad_heads(w_out[1][:MLA_HEADS * MLA_DV], MLA_HEADS, MLA_DV, HEAD_PAD, axis=0))
    wb_out_mem = bf(w_out[1][MLA_HEADS * MLA_DV:])
    wuk_h = kv_w_uk.reshape(MLA_KV_RANK, MLA_HEADS, MLA_DN).transpose(1, 2, 0)
    w_abs = bf(jnp.pad(wuk_h * kv_k_nope_norm[None, :, None], ((0, 0), (0, HEAD_PAD - MLA_DN), (0, 0))))
    w_uk_t = bf(kv_w_uk.T)
    w_uv_h = bf(jnp.pad(kv_w_uv.reshape(MLA_KV_RANK, MLA_HEADS, MLA_DV).transpose(1, 0, 2),
                        ((0, 0), (0, 0), (0, HEAD_PAD - MLA_DV))))

    mem_k_p, mem_v_p = _memkv(mem_prompt.reshape(bp * n_mem, D_MODEL), mem_norm.reshape(2, 1, D_MODEL),
                              bf(w_mem_k), bf(w_mem_v),
                              jnp.tile(mem_k_norm, (1, MEM_HEADS)).reshape(2, 1, MEM_W), ones64)
    mem_k_p = mem_k_p.reshape(2, bp, n_mem, MEM_W)
    mem_v_p = mem_v_p.reshape(2, bp, n_mem, MEM_W)

    def trunk(x, nb, t, pos, mem_k, mem_v, s0, attend):
        n = nb * t
        tm = _tile_for(n, FFN_ROWS)
        tp_ = _tile_for(n, PROJ_ROWS)
        ta = _tile_for(n, A_IN_ROWS)
        seq_pad = (-t) % T_PAD
        tpad = t + seq_pad

        def to_seq(a):
            a = a.reshape(nb, t, a.shape[-1])
            return jnp.pad(a, ((0, 0), (0, seq_pad), (0, 0))) if seq_pad else a

        def from_seq(a):
            return a[:, :t].reshape(n, a.shape[-1])

        cos, sin = _rope_tables(pos)
        cos = jnp.tile(cos, (nb, 1))
        sin = jnp.tile(sin, (nb, 1))

        def mem_part(mq, l):
            return from_seq(_mem_attn(to_seq(mq), mem_k[l], mem_v[l], _tile_for(tpad, 1024)))

        x = _ffn_half(x, *ffn[0][0], tm)
        q, k, v, la, r, mq = _a_in(x, row(mix_norm[0]), wa_in, w_gate2, b_gate, gmq[0], ones64, ta)
        s0t = jnp.pad(s0.transpose(0, 1, 3, 2), ((0, 0), (0, 0), (0, DV_PAD - GLA_DV), (0, DK_PAD - GLA_DK)))
        o, st = _gla(to_seq(q), to_seq(k), to_seq(v), to_seq(la), s0t, math.gcd(tpad, GLA_CHUNK))
        gla_state = st[:, :, :GLA_DV, :GLA_DK].transpose(0, 1, 3, 2)
        x = _a_out(x, from_seq(o), r, g_o, mem_part(mq, 0), wa_out_main, wa_out_mem, tp_)
        x = _ffn_half(x, *ffn[1][0], tm)
        c, kpe, k_full, v_full = _latent(x, row(kv_norm), w_lat, row(kv_ckv_norm), g_kpe_a, g_kpe_b, cos, sin,
                                         w_uk_pad, g_kn_slab, w_uv_pad, ones_head, tp_)
        kpe = kpe[:, MLA_DN:MLA_DN + MLA_DR]
        x = _ffn_half(x, *ffn[0][1], tm)
        q_full, mq = _b_in(x, row(mix_norm[1]), wb_in, row(mla_q_a_norm[0]), w_q1, w_q2, g_q1, g_q2, inv_sizes,
                           cos, sin, gmq[1], ones_head, ones64, tp_)
        o_main = attend(q_full, k_full, v_full, c, kpe)
        x = _b_out(x, o_main, mem_part(mq, 1), wb_out_main, wb_out_mem, tp_)
        x = _ffn_half(x, *ffn[1][1], tm)
        return x, gla_state, c, kpe

    def attend_prompt(q_full, k_full, v_full, c, kpe):
        sh = (bp, tp, KV_W)
        o = _flash(q_full.reshape(sh), k_full.reshape(sh), v_full.reshape(sh), _tile_for(tp, 512))
        return o.reshape(bp * tp, KV_W)

    def attend_sample(q_full, k_full, v_full, c, kpe):
        n = bs * ts
        qa = _absorb(q_full, w_abs)

        def rows(a):
            a = a.reshape(MLA_HEADS, bs, ts, a.shape[-1]).transpose(1, 0, 2, 3)
            a = jnp.pad(a, ((0, 0), (0, 0), (0, T_PAD - ts), (0, 0)))
            return a.reshape(bs, QROWS, a.shape[-1])

        qp = q_full.reshape(n, MLA_HEADS, HEAD_PAD)[:, :, MLA_DN:MLA_DN + MLA_DR].transpose(1, 0, 2)
        page = cache_ckv.shape[1]
        c_new = jnp.pad(c.reshape(bs, ts, MLA_KV_RANK), ((0, 0), (0, page - ts), (0, 0)))
        k_new = jnp.pad(kpe.reshape(bs, ts, MLA_DR), ((0, 0), (0, page - ts), (0, 0)))
        ctx = _sample_attn(page_table, cache_ckv, cache_kpe, rows(qa), rows(qp), w_uk_t, c_new, k_new)
        ctx = ctx.reshape(bs, MLA_HEADS, T_PAD, MLA_KV_RANK).transpose(1, 0, 2, 3)
        o = _uv_project(ctx.reshape(MLA_HEADS, bs * T_PAD, MLA_KV_RANK), w_uv_h)
        o = o.reshape(MLA_HEADS, bs, T_PAD, HEAD_PAD)[:, :, :ts].transpose(1, 2, 0, 3)
        return o.reshape(n, KV_W)

    mk_p = [mem_k_p[l] for l in range(2)]
    mv_p = [mem_v_p[l] for l in range(2)]
    s0_p = jnp.zeros((bp, GLA_HEADS, GLA_DK, GLA_DV), F32)
    y_p, st_p, c_p, kpe_p = trunk(x_prompt.reshape(bp * tp, D_MODEL), bp, tp, jnp.arange(tp), mk_p, mv_p,
                                  s0_p, attend_prompt)

    mk_s = [cache_mem_k[l].reshape(bs, n_mem, MEM_W) for l in range(2)]
    mv_s = [cache_mem_v[l].reshape(bs, n_mem, MEM_W) for l in range(2)]
    y_s, st_s, c_s, kpe_s = trunk(x_sample.reshape(bs * ts, D_MODEL), bs, ts, past + jnp.arange(ts), mk_s, mv_s,
                                  state_gla[0], attend_sample)

    return (y_p.reshape(bp, tp, D_MODEL), y_s.reshape(bs, ts, D_MODEL),
            st_p[None], st_s[None],
            c_p.reshape(bp, tp, MLA_KV_RANK), kpe_p.reshape(bp, tp, MLA_DR),
            c_s.reshape(bs, ts, MLA_KV_RANK), kpe_s.reshape(bs, ts, MLA_DR),
            mem_k_p.reshape(2, bp, n_mem, MEM_HEADS, MEM_DH), mem_v_p.reshape(2, bp, n_mem, MEM_HEADS, MEM_DH))
```

```python
import functools
import math

import jax
import jax.numpy as jnp
from jax import lax
from jax.experimental import pallas as pl
from jax.experimental.pallas import tpu as pltpu

F32 = jnp.float32
BF16 = jnp.bfloat16

D_MODEL = 1024
D_FF = 2816
GLA_HEADS = 4
GLA_DK = 96
GLA_DV = 192
GLA_QK = GLA_HEADS * GLA_DK
GLA_V = GLA_HEADS * GLA_DV
GLA_GATE_RANK = 16
GLA_GATE_TEMP = 16.0
MEM_HEADS = 4
MEM_DH = 64
MEM_W = MEM_HEADS * MEM_DH
MLA_HEADS = 12
MLA_DN = 64
MLA_DR = 32
MLA_DV = 64
MLA_KV_RANK = 256
MLA_Q_RANK = 384
ROPE_THETA = 10000.0
EPS = 1e-6

LANE = 128
DK_PAD = LANE
DV_PAD = 2 * LANE
HEAD_PAD = LANE
V_ONE = MLA_DV
LOG2E = math.log2(math.e)
GLA_CHUNK = 64
T_PAD = 8
NEG = -1e30
VMEM_LIMIT = 56 * 1024 * 1024
FFN_ROWS = 1024
PROJ_ROWS = 512
A_IN_ROWS = 256

NT_DIMS = (((1,), (1,)), ((), ()))
TN_DIMS = (((0,), (0,)), ((), ()))


def _dot(a, b):
    return jnp.dot(a, b, preferred_element_type=F32)


def _dot_nt(a, b):
    return lax.dot_general(a, b, NT_DIMS, preferred_element_type=F32)


def _dot_tn(a, b):
    return lax.dot_general(a, b, TN_DIMS, preferred_element_type=F32)


def _rms(x, g):
    return x * lax.rsqrt(jnp.mean(x * x, axis=-1, keepdims=True) + EPS) * g


def _split3(x):
    hi = x.astype(BF16)
    r1 = x - hi.astype(F32)
    mid = r1.astype(BF16)
    lo = (r1 - mid.astype(F32)).astype(BF16)
    return hi, mid, lo


def _group_sum(x, ones_bd):
    hi, mid, _ = _split3(x)
    return _dot(hi, ones_bd) + _dot(mid, ones_bd)


def _params(sem):
    return pltpu.CompilerParams(dimension_semantics=sem, vmem_limit_bytes=VMEM_LIMIT)


def _ffn_kernel(x_ref, g_ref, wg_ref, wu_ref, wd_ref, o_ref, h_sc, acc_sc):
    f = pl.program_id(1)

    @pl.when(f == 0)
    def _():
        h_sc[...] = _rms(x_ref[...], g_ref[...]).astype(BF16)
        acc_sc[...] = jnp.zeros_like(acc_sc)

    h = h_sc[...]
    gate = _dot(h, wg_ref[...])
    up = _dot(h, wu_ref[...])
    act = (gate * jax.nn.sigmoid(gate) * up).astype(BF16)
    acc_sc[...] += _dot(act, wd_ref[...])

    @pl.when(f == pl.num_programs(1) - 1)
    def _():
        o_ref[...] = x_ref[...] + 0.5 * acc_sc[...]


def _ffn_half(x, g, wg, wu, wd, tm):
    n = x.shape[0]
    tf = 256
    return pl.pallas_call(
        _ffn_kernel,
        out_shape=jax.ShapeDtypeStruct((n, D_MODEL), F32),
        grid=(n // tm, D_FF // tf),
        in_specs=[
            pl.BlockSpec((tm, D_MODEL), lambda i, f: (i, 0)),
            pl.BlockSpec((1, D_MODEL), lambda i, f: (0, 0)),
            pl.BlockSpec((D_MODEL, tf), lambda i, f: (0, f)),
            pl.BlockSpec((D_MODEL, tf), lambda i, f: (0, f)),
            pl.BlockSpec((tf, D_MODEL), lambda i, f: (f, 0)),
        ],
        out_specs=pl.BlockSpec((tm, D_MODEL), lambda i, f: (i, 0)),
        scratch_shapes=[pltpu.VMEM((tm, D_MODEL), BF16), pltpu.VMEM((tm, D_MODEL), F32)],
        compiler_params=_params(("parallel", "arbitrary")),
        name="ffn_half",
    )(x, g, wg, wu, wd)


A_Q0 = 0
A_K0 = A_Q0 + GLA_HEADS * DK_PAD
A_V0 = A_K0 + GLA_HEADS * DK_PAD
A_R0 = A_V0 + GLA_HEADS * DV_PAD
A_M0 = A_R0 + GLA_HEADS * DV_PAD
A_G0 = A_M0 + MEM_W
A_COLS = A_G0 + LANE


def _head_rms(x, ones_bd, size, g):
    return x * lax.rsqrt(_group_sum(x * x, ones_bd) * (1.0 / size) + EPS) * g


def _a_in_kernel(x_ref, g_ref, w_ref, w2_ref, b2_ref, gq_ref, ones_ref,
                 q_ref, k_ref, v_ref, la_ref, r_ref, mq_ref):
    u = _rms(x_ref[...], g_ref[...]).astype(BF16)
    p = _dot(u, w_ref[...])
    q_ref[...] = p[:, A_Q0:A_K0] * (GLA_DK ** -0.5)
    k_ref[...] = p[:, A_K0:A_V0]
    v_ref[...] = p[:, A_V0:A_R0]
    r = p[:, A_R0:A_M0]
    r_ref[...] = r * jax.nn.sigmoid(r)
    mq_ref[...] = _head_rms(p[:, A_M0:A_G0], ones_ref[...], MEM_DH, gq_ref[...])
    z = _dot(p[:, A_G0:A_COLS].astype(BF16), w2_ref[...]) + b2_ref[...]
    log_sig = jnp.minimum(z, 0.0) - jnp.log1p(jnp.exp(-jnp.abs(z)))
    la_ref[...] = log_sig * (1.0 / GLA_GATE_TEMP)


def _a_in(x, g, w, w2, b2, gq, ones64, tm):
    n = x.shape[0]
    qk = GLA_HEADS * DK_PAD
    vv = GLA_HEADS * DV_PAD
    row = lambda c: pl.BlockSpec((tm, c), lambda i: (i, 0))
    full = lambda a: pl.BlockSpec(a.shape, lambda i: (0,) * a.ndim)
    return pl.pallas_call(
        _a_in_kernel,
        out_shape=(jax.ShapeDtypeStruct((n, qk), F32), jax.ShapeDtypeStruct((n, qk), F32),
                   jax.ShapeDtypeStruct((n, vv), F32), jax.ShapeDtypeStruct((n, qk), F32),
                   jax.ShapeDtypeStruct((n, vv), F32), jax.ShapeDtypeStruct((n, MEM_W), F32)),
        grid=(n // tm,),
        in_specs=[row(D_MODEL), full(g), full(w), full(w2), full(b2), full(gq), full(ones64)],
        out_specs=(row(qk), row(qk), row(vv), row(qk), row(vv), row(MEM_W)),
        compiler_params=_params(("parallel",)),
        name="layer_a_in",
    )(x, g, w, w2, b2, gq, ones64)


def _gla_kernel(q_ref, k_ref, v_ref, la_ref, s0_ref, o_ref, s_out_ref, st_sc):
    c = pl.program_id(1)
    chunk = q_ref.shape[1]

    @pl.when(c == 0)
    def _():
        st_sc[...] = s0_ref[0]

    la = la_ref[0]
    t_idx = lax.broadcasted_iota(jnp.int32, (chunk, chunk), 0)
    s_idx = lax.broadcasted_iota(jnp.int32, (chunk, chunk), 1)
    causal = s_idx <= t_idx
    tri = jnp.where(causal, 1.0, 0.0).astype(BF16)
    hi, mid, lo = _split3(la)
    a_cum = _dot(tri, hi) + _dot(tri, mid) + _dot(tri, lo)
    a_last = a_cum[chunk - 1:chunk, :]
    e_q = jnp.exp(a_cum)
    e_k = jnp.exp(-a_cum)
    e_kd = jnp.exp(a_last - a_cum)
    e_last = jnp.exp(a_last)
    q = q_ref[0]
    k = k_ref[0]
    v = v_ref[0]
    for h in range(GLA_HEADS):
        ks = slice(h * DK_PAD, (h + 1) * DK_PAD)
        vs = slice(h * DV_PAD, (h + 1) * DV_PAD)
        qh = (q[:, ks] * e_q[:, ks]).astype(BF16)
        kh = (k[:, ks] * e_k[:, ks]).astype(BF16)
        kd = (k[:, ks] * e_kd[:, ks]).astype(BF16)
        vh = v[:, vs].astype(BF16)
        st = st_sc[h]
        scores = jnp.where(causal, _dot_nt(qh, kh), 0.0).astype(BF16)
        o_ref[0, :, vs] = _dot(scores, vh) + _dot_nt(qh, st.astype(BF16))
        st_sc[h] = st * e_last[:, ks] + _dot_tn(vh, kd)

    @pl.when(c == pl.num_programs(1) - 1)
    def _():
        s_out_ref[0] = st_sc[...]


def _gla(q, k, v, la, s0t, chunk):
    b, t, _ = q.shape
    qk = GLA_HEADS * DK_PAD
    vv = GLA_HEADS * DV_PAD
    seq = lambda c: pl.BlockSpec((1, chunk, c), lambda i, j: (i, j, 0))
    st = pl.BlockSpec((1, GLA_HEADS, DV_PAD, DK_PAD), lambda i, j: (i, 0, 0, 0))
    return pl.pallas_call(
        _gla_kernel,
        out_shape=(jax.ShapeDtypeStruct((b, t, vv), F32),
                   jax.ShapeDtypeStruct((b, GLA_HEADS, DV_PAD, DK_PAD), F32)),
        grid=(b, t // chunk),
        in_specs=[seq(qk), seq(qk), seq(vv), seq(qk), st],
        out_specs=(seq(vv), st),
        scratch_shapes=[pltpu.VMEM((GLA_HEADS, DV_PAD, DK_PAD), F32)],
        compiler_params=_params(("parallel", "arbitrary")),
        name="gla_recurrence",
    )(q, k, v, la, s0t)


def _mem_attn_kernel(q_ref, k_ref, v_ref, o_ref):
    q = q_ref[0]
    k = k_ref[0].astype(BF16)
    v = v_ref[0].astype(BF16)
    head = lax.broadcasted_iota(jnp.int32, (1, MEM_W), 1) // MEM_DH
    acc = jnp.zeros(q.shape, F32)
    for h in range(MEM_HEADS):
        sel = head == h
        qh = jnp.where(sel, q, 0.0).astype(BF16)
        s = _dot_nt(qh, k) * (MEM_DH ** -0.5)
        p = jnp.exp(s - jnp.max(s, axis=-1, keepdims=True))
        pv = _dot(p.astype(BF16), v) / jnp.sum(p, axis=-1, keepdims=True)
        acc = acc + jnp.where(sel, pv, 0.0)
    o_ref[0] = acc


def _mem_attn(q, mk, mv, tq):
    b, t, _ = q.shape
    m = mk.shape[1]
    return pl.pallas_call(
        _mem_attn_kernel,
        out_shape=jax.ShapeDtypeStruct((b, t, MEM_W), F32),
        grid=(b, t // tq),
        in_specs=[pl.BlockSpec((1, tq, MEM_W), lambda i, j: (i, j, 0)),
                  pl.BlockSpec((1, m, MEM_W), lambda i, j: (i, 0, 0)),
                  pl.BlockSpec((1, m, MEM_W), lambda i, j: (i, 0, 0))],
        out_specs=pl.BlockSpec((1, tq, MEM_W), lambda i, j: (i, j, 0)),
        compiler_params=_params(("parallel", "parallel")),
        name="mem_attention",
    )(q, mk, mv)


def _a_out_kernel(x_ref, o_ref, r_ref, go_ref, om_ref, wa_ref, wm_ref, y_ref):
    o = o_ref[...]
    parts = []
    for h in range(GLA_HEADS):
        oh = o[:, h * DV_PAD:(h + 1) * DV_PAD]
        ms = jnp.sum(oh * oh, axis=-1, keepdims=True) * (1.0 / GLA_DV)
        parts.append(oh * lax.rsqrt(ms + EPS))
    on = jnp.concatenate(parts, axis=-1) * go_ref[...]
    main = (on * r_ref[...]).astype(BF16)
    y_ref[...] = (x_ref[...] + _dot(main, wa_ref[...])
                  + _dot(om_ref[...].astype(BF16), wm_ref[...]))


def _a_out(x, o, r, go, om, wa, wm, tm):
    n = x.shape[0]
    row = lambda c: pl.BlockSpec((tm, c), lambda i: (i, 0))
    full = lambda a: pl.BlockSpec(a.shape, lambda i: (0,) * a.ndim)
    return pl.pallas_call(
        _a_out_kernel,
        out_shape=jax.ShapeDtypeStruct((n, D_MODEL), F32),
        grid=(n // tm,),
        in_specs=[row(D_MODEL), row(o.shape[1]), row(r.shape[1]), full(go), row(MEM_W),
                  full(wa), full(wm)],
        out_specs=row(D_MODEL),
        compiler_params=_params(("parallel",)),
        name="layer_a_out",
    )(x, o, r, go, om, wa, wm)


def _b_out_kernel(x_ref, o_ref, om_ref, wa_ref, wm_ref, y_ref):
    y_ref[...] = (x_ref[...] + _dot(o_ref[...].astype(BF16), wa_ref[...])
                  + _dot(om_ref[...].astype(BF16), wm_ref[...]))


def _b_out(x, o, om, wa, wm, tm):
    n = x.shape[0]
    row = lambda c: pl.BlockSpec((tm, c), lambda i: (i, 0))
    full = lambda a: pl.BlockSpec(a.shape, lambda i: (0,) * a.ndim)
    return pl.pallas_call(
        _b_out_kernel,
        out_shape=jax.ShapeDtypeStruct((n, D_MODEL), F32),
        grid=(n // tm,),
        in_specs=[row(D_MODEL), row(o.shape[1]), row(MEM_W), full(wa), full(wm)],
        out_specs=row(D_MODEL),
        compiler_params=_params(("parallel",)),
        name="layer_b_out",
    )(x, o, om, wa, wm)


L_C0 = 0
L_A0 = MLA_KV_RANK
L_B0 = L_A0 + LANE
L_COLS = L_B0 + LANE
KV_W = MLA_HEADS * HEAD_PAD


def _latent_kernel(x_ref, g_ref, wl_ref, gc_ref, ga_ref, gb_ref, cos_ref, sin_ref,
                   wuk_ref, gkn_ref, wuvt_ref, ones_ref,
                   c_ref, kpe_ref, kf_ref, vt_ref):
    hn = _rms(x_ref[...], g_ref[...]).astype(BF16)
    y = _dot(hn, wl_ref[...])
    c = _rms(y[:, L_C0:L_A0], gc_ref[...])
    c_ref[...] = c
    a = y[:, L_A0:L_B0]
    b = y[:, L_B0:L_COLS]
    r = lax.rsqrt(jnp.sum(a * a, axis=-1, keepdims=True) * (1.0 / MLA_DR) + EPS)
    kpe = a * r * ga_ref[...] * cos_ref[...] + b * r * gb_ref[...] * sin_ref[...]
    kpe_ref[...] = kpe
    cb = c.astype(BF16)
    kn = _dot(cb, wuk_ref[...])
    ones = ones_ref[...]
    gkn = gkn_ref[...]
    for h in range(MLA_HEADS):
        hs = slice(h * HEAD_PAD, (h + 1) * HEAD_PAD)
        knh = kn[:, hs]
        ms = _group_sum(knh * knh, ones) * (1.0 / MLA_DN)
        kf_ref[:, hs] = (knh * lax.rsqrt(ms + EPS) * gkn + kpe).astype(BF16)
    vt = _dot_nt(wuvt_ref[...], cb)
    slab_row = lax.broadcasted_iota(jnp.int32, vt.shape, 0) % HEAD_PAD
    vt_ref[...] = jnp.where(slab_row == V_ONE, 1.0, vt).astype(BF16)


def _latent(x, g, wl, gc, ga, gb, cos, sin, wuk, gkn, wuvt, ones_h, tm):
    n = x.shape[0]
    row = lambda c: pl.BlockSpec((tm, c), lambda i: (i, 0))
    full = lambda a: pl.BlockSpec(a.shape, lambda i: (0,) * a.ndim)
    return pl.pallas_call(
        _latent_kernel,
        out_shape=(jax.ShapeDtypeStruct((n, MLA_KV_RANK), F32), jax.ShapeDtypeStruct((n, LANE), F32),
                   jax.ShapeDtypeStruct((n, KV_W), BF16), jax.ShapeDtypeStruct((KV_W, n), BF16)),
        grid=(n // tm,),
        in_specs=[row(D_MODEL), full(g), full(wl), full(gc), full(ga), full(gb), row(LANE), row(LANE),
                  full(wuk), full(gkn), full(wuvt), full(ones_h)],
        out_specs=(row(MLA_KV_RANK), row(LANE), row(KV_W), pl.BlockSpec((KV_W, tm), lambda i: (0, i))),
        compiler_params=_params(("parallel",)),
        name="shared_latent",
    )(x, g, wl, gc, ga, gb, cos, sin, wuk, gkn, wuvt, ones_h)


B_COLS = MLA_Q_RANK + MEM_W


def _b_in_kernel(x_ref, g_ref, w_ref, gqa_ref, w1_ref, w2_ref, g1_ref, g2_ref, inv_ref,
                 cos_ref, sin_ref, gq_ref, onesh_ref, ones64_ref, q_ref, mq_ref):
    u = _rms(x_ref[...], g_ref[...]).astype(BF16)
    p = _dot(u, w_ref[...])
    cq = _rms(p[:, :MLA_Q_RANK], gqa_ref[...]).astype(BF16)
    q1 = _dot(cq, w1_ref[...])
    q2 = _dot(cq, w2_ref[...])
    onesh = onesh_ref[...]
    inv = inv_ref[...]
    c1 = g1_ref[...] * cos_ref[...]
    c2 = g2_ref[...] * sin_ref[...]
    for h in range(MLA_HEADS):
        hs = slice(h * HEAD_PAD, (h + 1) * HEAD_PAD)
        q1h = q1[:, hs]
        rs = lax.rsqrt(_group_sum(q1h * q1h, onesh) * inv + EPS)
        q_ref[:, hs] = ((q1h * c1 + q2[:, hs] * c2) * rs).astype(BF16)
    mq_ref[...] = _head_rms(p[:, MLA_Q_RANK:B_COLS], ones64_ref[...], MEM_DH, gq_ref[...])


def _b_in(x, g, w, gqa, w1, w2, g1, g2, inv, cos, sin, gq, ones_h, ones64, tm):
    n = x.shape[0]
    row = lambda c: pl.BlockSpec((tm, c), lambda i: (i, 0))
    full = lambda a: pl.BlockSpec(a.shape, lambda i: (0,) * a.ndim)
    return pl.pallas_call(
        _b_in_kernel,
        out_shape=(jax.ShapeDtypeStruct((n, KV_W), BF16), jax.ShapeDtypeStruct((n, MEM_W), F32)),
        grid=(n // tm,),
        in_specs=[row(D_MODEL), full(g), full(w), full(gqa), full(w1), full(w2), full(g1), full(g2),
                  full(inv), row(LANE), row(LANE), full(gq), full(ones_h), full(ones64)],
        out_specs=(row(KV_W), row(MEM_W)),
        compiler_params=_params(("parallel",)),
        name="layer_b_in",
    )(x, g, w, gqa, w1, w2, g1, g2, inv, cos, sin, gq, ones_h, ones64)


def _flash_kernel(q_ref, k_ref, vt_ref, o_ref, sa_sc, sb_sc, m_sc, acc_sc):
    qi = pl.program_id(2)
    tq = q_ref.shape[1]
    q = q_ref[0]

    def scores_t(kb):
        start = pl.multiple_of(kb * tq, tq)
        return _dot_nt(k_ref[0, pl.ds(start, tq), :], q)

    def accumulate(s, kb, diagonal):
        if diagonal:
            key = lax.broadcasted_iota(jnp.int32, (tq, tq), 0)
            qry = lax.broadcasted_iota(jnp.int32, (tq, tq), 1)
            s = jnp.where(key <= qry, s, NEG)
        m = m_sc[...]
        m_new = jnp.maximum(m, jnp.max(s, axis=0, keepdims=True))
        p = jnp.exp2(s - m_new).astype(BF16)
        start = pl.multiple_of(kb * tq, tq)
        acc_sc[...] = jnp.exp2(m - m_new) * acc_sc[...] + _dot(vt_ref[:, pl.ds(start, tq)], p)
        m_sc[...] = m_new

    def finish():
        acc = acc_sc[...]
        o_ref[0] = (acc / acc[V_ONE:V_ONE + 1, :]).T.astype(o_ref.dtype)

    m_sc[...] = jnp.full_like(m_sc, NEG)
    acc_sc[...] = jnp.zeros_like(acc_sc)
    sa_sc[...] = scores_t(0)

    def pair(i, carry):
        kb = 2 * i
        sb_sc[...] = scores_t(kb + 1)
        accumulate(sa_sc[...], kb, False)
        sa_sc[...] = scores_t(kb + 2)
        accumulate(sb_sc[...], kb + 1, False)
        return carry

    lax.fori_loop(0, qi // 2, pair, 0)

    @pl.when(qi % 2 == 1)
    def _():
        sb_sc[...] = scores_t(qi)
        accumulate(sa_sc[...], qi - 1, False)
        accumulate(sb_sc[...], qi, True)
        finish()

    @pl.when(qi % 2 == 0)
    def _():
        accumulate(sa_sc[...], qi, True)
        finish()


def _flash(q, k, vt, tq):
    b, t, _ = q.shape
    qspec = pl.BlockSpec((1, tq, HEAD_PAD), lambda bi, h, i: (bi, i, h))
    return pl.pallas_call(
        _flash_kernel,
        out_shape=jax.ShapeDtypeStruct((b, t, KV_W), BF16),
        grid=(b, MLA_HEADS, t // tq),
        in_specs=[qspec,
                  pl.BlockSpec((1, t, HEAD_PAD), lambda bi, h, i: (bi, 0, h)),
                  pl.BlockSpec((HEAD_PAD, t), lambda bi, h, i: (h, bi))],
        out_specs=qspec,
        scratch_shapes=[pltpu.VMEM((tq, tq), F32), pltpu.VMEM((tq, tq), F32),
                        pltpu.VMEM((1, tq), F32), pltpu.VMEM((HEAD_PAD, tq), F32)],
        compiler_params=_params(("parallel", "parallel", "arbitrary")),
        name="prompt_latent_attention",
    )(q, k, vt)


PAGES_PER_STEP = 16
QROWS = MLA_HEADS * T_PAD
SUB = 512


def _absorb_kernel(q_ref, w_ref, o_ref):
    o_ref[0] = _dot(q_ref[...], w_ref[0]).astype(BF16)


def _absorb(q, wabs):
    n = q.shape[0]
    return pl.pallas_call(
        _absorb_kernel,
        out_shape=jax.ShapeDtypeStruct((MLA_HEADS, n, MLA_KV_RANK), BF16),
        grid=(MLA_HEADS,),
        in_specs=[pl.BlockSpec((n, HEAD_PAD), lambda h: (0, h)),
                  pl.BlockSpec((1, HEAD_PAD, MLA_KV_RANK), lambda h: (h, 0, 0))],
        out_specs=pl.BlockSpec((1, n, MLA_KV_RANK), lambda h: (h, 0, 0)),
        compiler_params=_params(("parallel",)),
        name="absorb_w_uk",
    )(q, wabs)


def _scores_block(qa, qp, wukt, cb, kbt):
    kt = _dot_nt(wukt, cb)
    sn = _dot_nt(qa, cb)
    sp = _dot(qp, kbt)
    rows = []
    for h in range(MLA_HEADS):
        kth = kt[h * MLA_DN:(h + 1) * MLA_DN, :]
        r = lax.rsqrt(jnp.sum(kth * kth, axis=0, keepdims=True) * (1.0 / MLA_DN) + EPS)
        rows.append(sn[h * T_PAD:(h + 1) * T_PAD, :] * r)
    return jnp.concatenate(rows, axis=0) + sp


def _sample_attn_kernel(pt_ref, *refs):
    npg = (len(refs) - 11) // 2
    c_refs = refs[:npg]
    k_refs = refs[npg:2 * npg]
    qa_ref, qp_ref, wukt_ref, cn_ref, kn_ref, o_ref, cb_sc, kb_sc, m_sc, l_sc, acc_sc = refs[2 * npg:]
    j = pl.program_id(1)
    page = c_refs[0].shape[1]

    @pl.when(j == 0)
    def _():
        m_sc[...] = jnp.full_like(m_sc, NEG)
        l_sc[...] = jnp.zeros_like(l_sc)
        acc_sc[...] = jnp.zeros_like(acc_sc)

    qa = qa_ref[0]
    qp = qp_ref[0]
    wukt = wukt_ref[...]

    def online(s, cb, m, l, acc):
        m_new = jnp.maximum(m, jnp.max(s, axis=-1, keepdims=True))
        alpha = jnp.exp2(m - m_new)
        p = jnp.exp2(s - m_new)
        l = alpha * l + jnp.sum(p, axis=-1, keepdims=True)
        acc = alpha * acc + _dot(p.astype(BF16), cb)
        return m_new, l, acc

    for i in range(npg):
        cb_sc[i * page:(i + 1) * page, :] = c_refs[i][0].astype(BF16)
        kb_sc[:, i * page:(i + 1) * page] = k_refs[i][0].astype(BF16)
    state = (m_sc[...], l_sc[...], acc_sc[...])
    sub = min(SUB, npg * page)
    for i in range(npg * page // sub):
        sl = slice(i * sub, (i + 1) * sub)
        cb = cb_sc[sl, :]
        state = online(_scores_block(qa, qp, wukt, cb, kb_sc[:, sl]), cb, *state)
    m_sc[...], l_sc[...], acc_sc[...] = state

    @pl.when(j == pl.num_programs(1) - 1)
    def _():
        cn = cn_ref[0].astype(BF16)
        s = _scores_block(qa, qp, wukt, cn, kn_ref[0].astype(BF16))
        t_q = lax.broadcasted_iota(jnp.int32, s.shape, 0) % T_PAD
        t_k = lax.broadcasted_iota(jnp.int32, s.shape, 1)
        _, l, acc = online(jnp.where(t_k <= t_q, s, NEG), cn, *state)
        o_ref[0] = acc / l


def _sample_attn(page_table, cache_ckv, cache_kpe_t, qa, qp, wukt, c_new, k_new_t):
    nb, n_pages = page_table.shape
    page = cache_ckv.shape[1]
    npg = math.gcd(n_pages, PAGES_PER_STEP)
    steps = n_pages // npg

    def pspec(i, shape):
        return pl.BlockSpec((1,) + shape, lambda b, j, pt: (pt[b, j * npg + i], 0, 0))

    per_b = lambda r, w: pl.BlockSpec((1, r, w), lambda b, j, pt: (b, 0, 0))
    in_specs = ([pspec(i, (page, MLA_KV_RANK)) for i in range(npg)]
                + [pspec(i, (MLA_DR, page)) for i in range(npg)]
                + [per_b(QROWS, MLA_KV_RANK), per_b(QROWS, MLA_DR),
                   pl.BlockSpec(wukt.shape, lambda b, j, pt: (0, 0)),
                   per_b(c_new.shape[1], MLA_KV_RANK), per_b(MLA_DR, k_new_t.shape[2])])
    return pl.pallas_call(
        _sample_attn_kernel,
        out_shape=jax.ShapeDtypeStruct((nb, QROWS, MLA_KV_RANK), F32),
        grid_spec=pltpu.PrefetchScalarGridSpec(
            num_scalar_prefetch=1,
            grid=(nb, steps),
            in_specs=in_specs,
            out_specs=per_b(QROWS, MLA_KV_RANK),
            scratch_shapes=[pltpu.VMEM((npg * page, MLA_KV_RANK), BF16),
                            pltpu.VMEM((MLA_DR, npg * page), BF16),
                            pltpu.VMEM((QROWS, 1), F32), pltpu.VMEM((QROWS, 1), F32),
                            pltpu.VMEM((QROWS, MLA_KV_RANK), F32)],
        ),
        compiler_params=_params(("parallel", "arbitrary")),
        name="sample_latent_attention",
    )(page_table, *([cache_ckv] * npg), *([cache_kpe_t] * npg), qa, qp, wukt, c_new, k_new_t)


def _uv_kernel(x_ref, w_ref, o_ref):
    o_ref[0] = _dot(x_ref[0].astype(BF16), w_ref[0])


def _uv_project(ctx, wuv):
    _, n, _ = ctx.shape
    return pl.pallas_call(
        _uv_kernel,
        out_shape=jax.ShapeDtypeStruct((MLA_HEADS, n, HEAD_PAD), F32),
        grid=(MLA_HEADS,),
        in_specs=[pl.BlockSpec((1, n, MLA_KV_RANK), lambda h: (h, 0, 0)),
                  pl.BlockSpec((1, MLA_KV_RANK, HEAD_PAD), lambda h: (h, 0, 0))],
        out_specs=pl.BlockSpec((1, n, HEAD_PAD), lambda h: (h, 0, 0)),
        compiler_params=_params(("parallel",)),
        name="apply_w_uv",
    )(ctx, wuv)


def _memkv_kernel(m_ref, g_ref, wk_ref, wv_ref, gk_ref, ones_ref, k_ref, v_ref):
    mn = _rms(m_ref[...], g_ref[0]).astype(BF16)
    k_ref[0] = _head_rms(_dot(mn, wk_ref[0]), ones_ref[...], MEM_DH, gk_ref[0])
    v_ref[0] = _dot(mn, wv_ref[0])


def _memkv(mem, g, wk, wv, gk, ones64):
    n = mem.shape[0]
    nl = wk.shape[0]
    lay = lambda a: pl.BlockSpec((1,) + a.shape[1:], lambda l: (l,) + (0,) * (a.ndim - 1))
    out = pl.BlockSpec((1, n, MEM_W), lambda l: (l, 0, 0))
    return pl.pallas_call(
        _memkv_kernel,
        out_shape=(jax.ShapeDtypeStruct((nl, n, MEM_W), F32), jax.ShapeDtypeStruct((nl, n, MEM_W), F32)),
        grid=(nl,),
        in_specs=[pl.BlockSpec(mem.shape, lambda l: (0, 0)), lay(g), lay(wk), lay(wv), lay(gk),
                  pl.BlockSpec(ones64.shape, lambda l: (0, 0))],
        out_specs=(out, out),
        compiler_params=_params(("parallel",)),
        name="prompt_mem_kv",
    )(mem, g, wk, wv, gk, ones64)


def _pad_heads(w, heads, size, padded, axis=-1):
    axis = axis % w.ndim
    shape = w.shape[:axis] + (heads, size) + w.shape[axis + 1:]
    w = w.reshape(shape)
    pad = [(0, 0)] * w.ndim
    pad[axis + 1] = (0, padded - size)
    w = jnp.pad(w, pad)
    return w.reshape(w.shape[:axis] + (heads * padded,) + w.shape[axis + 2:])


def _block_ones(n, sizes):
    gid = []
    g = 0
    while len(gid) < n:
        for s in sizes:
            gid += [g] * s
            g += 1
    gid = jnp.asarray(gid[:n], jnp.int32)
    return (gid[:, None] == gid[None, :]).astype(BF16)


def _rope_tables(pos):
    half = MLA_DR // 2
    inv = ROPE_THETA ** (-jnp.arange(half, dtype=F32) / half)
    ang = pos.astype(F32)[:, None] * inv
    cos, sin = jnp.cos(ang), jnp.sin(ang)
    z = jnp.zeros((pos.shape[0], MLA_DN), F32)
    z2 = jnp.zeros((pos.shape[0], HEAD_PAD - MLA_DN - MLA_DR), F32)
    return (jnp.concatenate([z + 1.0, cos, cos, z2], axis=-1), jnp.concatenate([z, -sin, sin, z2], axis=-1))


def _swap_halves(w):
    half = w.shape[-1] // 2
    return jnp.concatenate([w[..., half:], w[..., :half]], axis=-1)


def _rope_slab(w):
    pad = [(0, 0)] * (w.ndim - 1) + [(MLA_DN, HEAD_PAD - MLA_DN - MLA_DR)]
    return jnp.pad(w, pad)


def _tile_for(n, pref):
    t = min(n, pref)
    while n % t:
        t //= 2
    return t


def kernel(x_prompt, x_sample, mem_prompt, cache_mem_k, cache_mem_v, state_gla, cache_ckv, cache_kpe, page_table,
           ffn1_norm, ffn1_w_gate, ffn1_w_up, ffn1_w_down, ffn2_norm, ffn2_w_gate, ffn2_w_up, ffn2_w_down,
           mix_norm, w_out, mem_norm, w_mem_k, w_mem_v, mem_k_norm, mem_q_norm,
           gla_w_in, gla_w_gate2, gla_b_gate, gla_o_norm,
           mla_w_in, mla_q_a_norm, mla_w_uq, mla_q_nope_norm, mla_q_pe_norm,
           kv_norm, kv_w_dkv, kv_ckv_norm, kv_w_kr, kv_kpe_norm, kv_w_uk, kv_k_nope_norm, kv_w_uv):
    bp, tp, _ = x_prompt.shape
    bs, ts, _ = x_sample.shape
    n_mem = mem_prompt.shape[1]
    past = page_table.shape[1] * cache_ckv.shape[1]
    row = lambda a: a.reshape(1, -1).astype(F32)
    bf = lambda a: a.astype(BF16)

    ffn = [[(row(n[l]), bf(g[l]), bf(u[l]), bf(d[l])) for l in range(2)]
           for n, g, u, d in ((ffn1_norm, ffn1_w_gate, ffn1_w_up, ffn1_w_down),
                              (ffn2_norm, ffn2_w_gate, ffn2_w_up, ffn2_w_down))]
    ones64 = _block_ones(MEM_W, [MEM_DH])
    ones_head = _block_ones(HEAD_PAD, [MLA_DN, MLA_DR, HEAD_PAD - MLA_DN - MLA_DR])
    gmq = [row(jnp.tile(mem_q_norm[l], MEM_HEADS)) for l in range(2)]

    wi = gla_w_in[0]
    o_q, o_k, o_v, o_g, o_r, o_m = 0, GLA_QK, 2 * GLA_QK, 2 * GLA_QK + GLA_V, 2 * GLA_QK + GLA_V + GLA_GATE_RANK, \
        2 * GLA_QK + 2 * GLA_V + GLA_GATE_RANK
    wa_in = bf(jnp.concatenate([
        _pad_heads(wi[:, o_q:o_k], GLA_HEADS, GLA_DK, DK_PAD),
        _pad_heads(wi[:, o_k:o_v], GLA_HEADS, GLA_DK, DK_PAD),
        _pad_heads(wi[:, o_v:o_g], GLA_HEADS, GLA_DV, DV_PAD),
        _pad_heads(wi[:, o_r:o_m], GLA_HEADS, GLA_DV, DV_PAD),
        wi[:, o_m:],
        jnp.pad(wi[:, o_g:o_r], ((0, 0), (0, LANE - GLA_GATE_RANK))),
    ], axis=1))
    w_gate2 = bf(jnp.pad(_pad_heads(gla_w_gate2[0], GLA_HEADS, GLA_DK, DK_PAD),
                         ((0, LANE - GLA_GATE_RANK), (0, 0))))
    b_gate = row(_pad_heads(gla_b_gate[0], GLA_HEADS, GLA_DK, DK_PAD))
    g_o = row(jnp.pad(gla_o_norm[0], (0, DV_PAD - GLA_DV)))
    g_o = jnp.tile(g_o, (1, GLA_HEADS))
    wa_out_main = bf(_pad_heads(w_out[0][:GLA_V], GLA_HEADS, GLA_DV, DV_PAD, axis=0))
    wa_out_mem = bf(w_out[0][GLA_V:])

    kr_slab = _rope_slab(kv_w_kr)
    w_lat = bf(jnp.concatenate([kv_w_dkv, kr_slab, _rope_slab(_swap_halves(kv_w_kr))], axis=1))
    g_kpe_a = row(_rope_slab(kv_kpe_norm))
    g_kpe_b = row(_rope_slab(_swap_halves(kv_kpe_norm)))
    w_uk_pad = bf(_pad_heads(kv_w_uk, MLA_HEADS, MLA_DN, HEAD_PAD))
    w_uv_t = bf(_pad_heads(kv_w_uv, MLA_HEADS, MLA_DV, HEAD_PAD).T)
    g_kn_slab = row(jnp.pad(kv_k_nope_norm, (0, HEAD_PAD - MLA_DN)))

    scale = (MLA_DN + MLA_DR) ** -0.5 * LOG2E
    wuq = mla_w_uq[0].reshape(MLA_Q_RANK, MLA_HEADS, MLA_DN + MLA_DR)
    w_q1 = bf(jnp.pad(wuq, ((0, 0), (0, 0), (0, HEAD_PAD - MLA_DN - MLA_DR))).reshape(MLA_Q_RANK, KV_W))
    w_q2 = bf(_rope_slab(_swap_halves(wuq[..., MLA_DN:])).reshape(MLA_Q_RANK, KV_W))
    g_q1 = row(jnp.concatenate([mla_q_nope_norm[0], mla_q_pe_norm[0],
                                jnp.zeros((HEAD_PAD - MLA_DN - MLA_DR,), F32)])) * scale
    g_q2 = row(_rope_slab(_swap_halves(mla_q_pe_norm[0]))) * scale
    inv_sizes = row(jnp.concatenate([jnp.full((MLA_DN,), 1.0 / MLA_DN, F32),
                                     jnp.full((HEAD_PAD - MLA_DN,), 1.0 / MLA_DR, F32)]))
    wb_in = bf(mla_w_in[0])
    wb_out_main = bf(_pad_heads(w_out[1][:MLA_HEADS * MLA_DV], MLA_HEADS, MLA_DV, HEAD_PAD, axis=0))
    wb_out_mem = bf(w_out[1][MLA_HEADS * MLA_DV:])
    wuk_h = kv_w_uk.reshape(MLA_KV_RANK, MLA_HEADS, MLA_DN).transpose(1, 2, 0)
    w_abs = bf(jnp.pad(wuk_h * kv_k_nope_norm[None, :, None], ((0, 0), (0, HEAD_PAD - MLA_DN), (0, 0))))
    w_uk_t = bf(kv_w_uk.T)
    w_uv_h = bf(jnp.pad(kv_w_uv.reshape(MLA_KV_RANK, MLA_HEADS, MLA_DV).transpose(1, 0, 2),
                        ((0, 0), (0, 0), (0, HEAD_PAD - MLA_DV))))

    mem_k_p, mem_v_p = _memkv(mem_prompt.reshape(bp * n_mem, D_MODEL), mem_norm.reshape(2, 1, D_MODEL),
                              bf(w_mem_k), bf(w_mem_v),
                              jnp.tile(mem_k_norm, (1, MEM_HEADS)).reshape(2, 1, MEM_W), ones64)
    mem_k_p = mem_k_p.reshape(2, bp, n_mem, MEM_W)
    mem_v_p = mem_v_p.reshape(2, bp, n_mem, MEM_W)

    def trunk(x, nb, t, pos, mem_k, mem_v, s0, attend):
        n = nb * t
        tm = _tile_for(n, FFN_ROWS)
        tp_ = _tile_for(n, PROJ_ROWS)
        ta = _tile_for(n, A_IN_ROWS)
        seq_pad = (-t) % T_PAD
        tpad = t + seq_pad

        def to_seq(a):
            a = a.reshape(nb, t, a.shape[-1])
            return jnp.pad(a, ((0, 0), (0, seq_pad), (0, 0))) if seq_pad else a

        def from_seq(a):
            return a[:, :t].reshape(n, a.shape[-1])

        cos, sin = _rope_tables(pos)
        cos = jnp.tile(cos, (nb, 1))
        sin = jnp.tile(sin, (nb, 1))

        def mem_part(mq, l):
            return from_seq(_mem_attn(to_seq(mq), mem_k[l], mem_v[l], _tile_for(tpad, 1024)))

        x = _ffn_half(x, *ffn[0][0], tm)
        q, k, v, la, r, mq = _a_in(x, row(mix_norm[0]), wa_in, w_gate2, b_gate, gmq[0], ones64, ta)
        s0t = jnp.pad(s0.transpose(0, 1, 3, 2), ((0, 0), (0, 0), (0, DV_PAD - GLA_DV), (0, DK_PAD - GLA_DK)))
        o, st = _gla(to_seq(q), to_seq(k), to_seq(v), to_seq(la), s0t, math.gcd(tpad, GLA_CHUNK))
        gla_state = st[:, :, :GLA_DV, :GLA_DK].transpose(0, 1, 3, 2)
        x = _a_out(x, from_seq(o), r, g_o, mem_part(mq, 0), wa_out_main, wa_out_mem, tp_)
        x = _ffn_half(x, *ffn[1][0], tm)
        c, kpe, k_full, v_full = _latent(x, row(kv_norm), w_lat, row(kv_ckv_norm), g_kpe_a, g_kpe_b, cos, sin,
                                         w_uk_pad, g_kn_slab, w_uv_t, ones_head, tp_)
        kpe = kpe[:, MLA_DN:MLA_DN + MLA_DR]
        x = _ffn_half(x, *ffn[0][1], tm)
        q_full, mq = _b_in(x, row(mix_norm[1]), wb_in, row(mla_q_a_norm[0]), w_q1, w_q2, g_q1, g_q2, inv_sizes,
                           cos, sin, gmq[1], ones_head, ones64, tp_)
        o_main = attend(q_full, k_full, v_full, c, kpe)
        x = _b_out(x, o_main, mem_part(mq, 1), wb_out_main, wb_out_mem, tp_)
        x = _ffn_half(x, *ffn[1][1], tm)
        return x, gla_state, c, kpe

    def attend_prompt(q_full, k_full, v_full, c, kpe):
        sh = (bp, tp, KV_W)
        o = _flash(q_full.reshape(sh), k_full.reshape(sh), v_full, _tile_for(tp, 512))
        return o.reshape(bp * tp, KV_W)

    def attend_sample(q_full, k_full, v_full, c, kpe):
        n = bs * ts
        qa = _absorb(q_full, w_abs)

        def rows(a):
            a = a.reshape(MLA_HEADS, bs, ts, a.shape[-1]).transpose(1, 0, 2, 3)
            a = jnp.pad(a, ((0, 0), (0, 0), (0, T_PAD - ts), (0, 0)))
            return a.reshape(bs, QROWS, a.shape[-1])

        qp = q_full.reshape(n, MLA_HEADS, HEAD_PAD)[:, :, MLA_DN:MLA_DN + MLA_DR].transpose(1, 0, 2)
        page = cache_ckv.shape[1]
        c_new = jnp.pad(c.reshape(bs, ts, MLA_KV_RANK), ((0, 0), (0, page - ts), (0, 0)))
        k_new_t = jnp.pad(kpe.reshape(bs, ts, MLA_DR), ((0, 0), (0, page - ts), (0, 0))).swapaxes(1, 2)
        ctx = _sample_attn(page_table, cache_ckv, cache_kpe.swapaxes(1, 2), rows(qa), rows(qp), w_uk_t,
                           c_new, k_new_t)
        ctx = ctx.reshape(bs, MLA_HEADS, T_PAD, MLA_KV_RANK).transpose(1, 0, 2, 3)
        o = _uv_project(ctx.reshape(MLA_HEADS, bs * T_PAD, MLA_KV_RANK), w_uv_h)
        o = o.reshape(MLA_HEADS, bs, T_PAD, HEAD_PAD)[:, :, :ts].transpose(1, 2, 0, 3)
        return o.reshape(n, KV_W)

    mk_p = [mem_k_p[l] for l in range(2)]
    mv_p = [mem_v_p[l] for l in range(2)]
    s0_p = jnp.zeros((bp, GLA_HEADS, GLA_DK, GLA_DV), F32)
    y_p, st_p, c_p, kpe_p = trunk(x_prompt.reshape(bp * tp, D_MODEL), bp, tp, jnp.arange(tp), mk_p, mv_p,
                                  s0_p, attend_prompt)

    mk_s = [cache_mem_k[l].reshape(bs, n_mem, MEM_W) for l in range(2)]
    mv_s = [cache_mem_v[l].reshape(bs, n_mem, MEM_W) for l in range(2)]
    y_s, st_s, c_s, kpe_s = trunk(x_sample.reshape(bs * ts, D_MODEL), bs, ts, past + jnp.arange(ts), mk_s, mv_s,
                                  state_gla[0], attend_sample)

    return (y_p.reshape(bp, tp, D_MODEL), y_s.reshape(bs, ts, D_MODEL),
            st_p[None], st_s[None],
            c_p.reshape(bp, tp, MLA_KV_RANK), kpe_p.reshape(bp, tp, MLA_DR),
            c_s.reshape(bs, ts, MLA_KV_RANK), kpe_s.reshape(bs, ts, MLA_DR),
            mem_k_p.reshape(2, bp, n_mem, MEM_HEADS, MEM_DH), mem_v_p.reshape(2, bp, n_mem, MEM_HEADS, MEM_DH))
```

```python
import functools
import math

import jax
import jax.numpy as jnp
from jax import lax
from jax.experimental import pallas as pl
from jax.experimental.pallas import tpu as pltpu

F32 = jnp.float32
BF16 = jnp.bfloat16

D_MODEL = 1024
D_FF = 2816
GLA_HEADS = 4
GLA_DK = 96
GLA_DV = 192
GLA_QK = GLA_HEADS * GLA_DK
GLA_V = GLA_HEADS * GLA_DV
GLA_GATE_RANK = 16
GLA_GATE_TEMP = 16.0
MEM_HEADS = 4
MEM_DH = 64
MEM_W = MEM_HEADS * MEM_DH
MLA_HEADS = 12
MLA_DN = 64
MLA_DR = 32
MLA_DV = 64
MLA_KV_RANK = 256
MLA_Q_RANK = 384
ROPE_THETA = 10000.0
EPS = 1e-6

LANE = 128
DK_PAD = LANE
DV_PAD = 2 * LANE
HEAD_PAD = LANE
V_ONE = MLA_DV
LOG2E = math.log2(math.e)
GLA_CHUNK = 64
T_PAD = 8
NEG = -1e30
VMEM_LIMIT = 56 * 1024 * 1024
FFN_ROWS = 1024
PROJ_ROWS = 512
A_IN_ROWS = 256

NT_DIMS = (((1,), (1,)), ((), ()))
TN_DIMS = (((0,), (0,)), ((), ()))


def _dot(a, b):
    return jnp.dot(a, b, preferred_element_type=F32)


def _dot_nt(a, b):
    return lax.dot_general(a, b, NT_DIMS, preferred_element_type=F32)


def _dot_tn(a, b):
    return lax.dot_general(a, b, TN_DIMS, preferred_element_type=F32)


def _rms(x, g):
    return x * lax.rsqrt(jnp.mean(x * x, axis=-1, keepdims=True) + EPS) * g


def _split3(x):
    hi = x.astype(BF16)
    r1 = x - hi.astype(F32)
    mid = r1.astype(BF16)
    lo = (r1 - mid.astype(F32)).astype(BF16)
    return hi, mid, lo


def _group_sum(x, ones_bd):
    hi, mid, _ = _split3(x)
    return _dot(hi, ones_bd) + _dot(mid, ones_bd)


def _params(sem):
    return pltpu.CompilerParams(dimension_semantics=sem, vmem_limit_bytes=VMEM_LIMIT)


def _ffn_kernel(x_ref, g_ref, wg_ref, wu_ref, wd_ref, o_ref, h_sc, acc_sc):
    f = pl.program_id(1)

    @pl.when(f == 0)
    def _():
        h_sc[...] = _rms(x_ref[...], g_ref[...]).astype(BF16)
        acc_sc[...] = jnp.zeros_like(acc_sc)

    h = h_sc[...]
    gate = _dot(h, wg_ref[...])
    up = _dot(h, wu_ref[...])
    act = (gate * jax.nn.sigmoid(gate) * up).astype(BF16)
    acc_sc[...] += _dot(act, wd_ref[...])

    @pl.when(f == pl.num_programs(1) - 1)
    def _():
        o_ref[...] = x_ref[...] + 0.5 * acc_sc[...]


def _ffn_half(x, g, wg, wu, wd, tm):
    n = x.shape[0]
    tf = 256
    return pl.pallas_call(
        _ffn_kernel,
        out_shape=jax.ShapeDtypeStruct((n, D_MODEL), F32),
        grid=(n // tm, D_FF // tf),
        in_specs=[
            pl.BlockSpec((tm, D_MODEL), lambda i, f: (i, 0)),
            pl.BlockSpec((1, D_MODEL), lambda i, f: (0, 0)),
            pl.BlockSpec((D_MODEL, tf), lambda i, f: (0, f)),
            pl.BlockSpec((D_MODEL, tf), lambda i, f: (0, f)),
            pl.BlockSpec((tf, D_MODEL), lambda i, f: (f, 0)),
        ],
        out_specs=pl.BlockSpec((tm, D_MODEL), lambda i, f: (i, 0)),
        scratch_shapes=[pltpu.VMEM((tm, D_MODEL), BF16), pltpu.VMEM((tm, D_MODEL), F32)],
        compiler_params=_params(("parallel", "arbitrary")),
        name="ffn_half",
    )(x, g, wg, wu, wd)


A_Q0 = 0
A_K0 = A_Q0 + GLA_HEADS * DK_PAD
A_V0 = A_K0 + GLA_HEADS * DK_PAD
A_R0 = A_V0 + GLA_HEADS * DV_PAD
A_M0 = A_R0 + GLA_HEADS * DV_PAD
A_G0 = A_M0 + MEM_W
A_COLS = A_G0 + LANE


def _head_rms(x, ones_bd, size, g):
    return x * lax.rsqrt(_group_sum(x * x, ones_bd) * (1.0 / size) + EPS) * g


def _a_in_kernel(x_ref, g_ref, w_ref, w2_ref, b2_ref, gq_ref, ones_ref,
                 q_ref, k_ref, v_ref, la_ref, r_ref, mq_ref):
    u = _rms(x_ref[...], g_ref[...]).astype(BF16)
    p = _dot(u, w_ref[...])
    q_ref[...] = p[:, A_Q0:A_K0] * (GLA_DK ** -0.5)
    k_ref[...] = p[:, A_K0:A_V0]
    v_ref[...] = p[:, A_V0:A_R0]
    r = p[:, A_R0:A_M0]
    r_ref[...] = r * jax.nn.sigmoid(r)
    mq_ref[...] = _head_rms(p[:, A_M0:A_G0], ones_ref[...], MEM_DH, gq_ref[...])
    z = _dot(p[:, A_G0:A_COLS].astype(BF16), w2_ref[...]) + b2_ref[...]
    log_sig = jnp.minimum(z, 0.0) - jnp.log1p(jnp.exp(-jnp.abs(z)))
    la_ref[...] = log_sig * (1.0 / GLA_GATE_TEMP)


def _a_in(x, g, w, w2, b2, gq, ones64, tm):
    n = x.shape[0]
    qk = GLA_HEADS * DK_PAD
    vv = GLA_HEADS * DV_PAD
    row = lambda c: pl.BlockSpec((tm, c), lambda i: (i, 0))
    full = lambda a: pl.BlockSpec(a.shape, lambda i: (0,) * a.ndim)
    return pl.pallas_call(
        _a_in_kernel,
        out_shape=(jax.ShapeDtypeStruct((n, qk), F32), jax.ShapeDtypeStruct((n, qk), F32),
                   jax.ShapeDtypeStruct((n, vv), F32), jax.ShapeDtypeStruct((n, qk), F32),
                   jax.ShapeDtypeStruct((n, vv), F32), jax.ShapeDtypeStruct((n, MEM_W), F32)),
        grid=(n // tm,),
        in_specs=[row(D_MODEL), full(g), full(w), full(w2), full(b2), full(gq), full(ones64)],
        out_specs=(row(qk), row(qk), row(vv), row(qk), row(vv), row(MEM_W)),
        compiler_params=_params(("parallel",)),
        name="layer_a_in",
    )(x, g, w, w2, b2, gq, ones64)


def _gla_kernel(q_ref, k_ref, v_ref, la_ref, s0_ref, o_ref, s_out_ref, st_sc):
    c = pl.program_id(1)
    chunk = q_ref.shape[1]

    @pl.when(c == 0)
    def _():
        st_sc[...] = s0_ref[0]

    la = la_ref[0]
    t_idx = lax.broadcasted_iota(jnp.int32, (chunk, chunk), 0)
    s_idx = lax.broadcasted_iota(jnp.int32, (chunk, chunk), 1)
    causal = s_idx <= t_idx
    tri = jnp.where(causal, 1.0, 0.0).astype(BF16)
    hi, mid, lo = _split3(la)
    a_cum = _dot(tri, hi) + _dot(tri, mid) + _dot(tri, lo)
    a_last = a_cum[chunk - 1:chunk, :]
    e_q = jnp.exp(a_cum)
    e_k = jnp.exp(-a_cum)
    e_kd = jnp.exp(a_last - a_cum)
    e_last = jnp.exp(a_last)
    q = q_ref[0]
    k = k_ref[0]
    v = v_ref[0]
    for h in range(GLA_HEADS):
        ks = slice(h * DK_PAD, (h + 1) * DK_PAD)
        vs = slice(h * DV_PAD, (h + 1) * DV_PAD)
        qh = (q[:, ks] * e_q[:, ks]).astype(BF16)
        kh = (k[:, ks] * e_k[:, ks]).astype(BF16)
        kd = (k[:, ks] * e_kd[:, ks]).astype(BF16)
        vh = v[:, vs].astype(BF16)
        st = st_sc[h]
        scores = jnp.where(causal, _dot_nt(qh, kh), 0.0).astype(BF16)
        o_ref[0, :, vs] = _dot(scores, vh) + _dot_nt(qh, st.astype(BF16))
        st_sc[h] = st * e_last[:, ks] + _dot_tn(vh, kd)

    @pl.when(c == pl.num_programs(1) - 1)
    def _():
        s_out_ref[0] = st_sc[...]


def _gla(q, k, v, la, s0t, chunk):
    b, t, _ = q.shape
    qk = GLA_HEADS * DK_PAD
    vv = GLA_HEADS * DV_PAD
    seq = lambda c: pl.BlockSpec((1, chunk, c), lambda i, j: (i, j, 0))
    st = pl.BlockSpec((1, GLA_HEADS, DV_PAD, DK_PAD), lambda i, j: (i, 0, 0, 0))
    return pl.pallas_call(
        _gla_kernel,
        out_shape=(jax.ShapeDtypeStruct((b, t, vv), F32),
                   jax.ShapeDtypeStruct((b, GLA_HEADS, DV_PAD, DK_PAD), F32)),
        grid=(b, t // chunk),
        in_specs=[seq(qk), seq(qk), seq(vv), seq(qk), st],
        out_specs=(seq(vv), st),
        scratch_shapes=[pltpu.VMEM((GLA_HEADS, DV_PAD, DK_PAD), F32)],
        compiler_params=_params(("parallel", "arbitrary")),
        name="gla_recurrence",
    )(q, k, v, la, s0t)


def _mem_attn_kernel(q_ref, kt_ref, vt_ref, o_ref):
    q = q_ref[0]
    kt = kt_ref[0].astype(BF16)
    vt = vt_ref[0].astype(BF16)
    head = lax.broadcasted_iota(jnp.int32, (1, MEM_W), 1) // MEM_DH
    acc = jnp.zeros(q.shape, F32)
    for h in range(MEM_HEADS):
        sel = head == h
        qh = jnp.where(sel, q, 0.0).astype(BF16)
        s = _dot(qh, kt) * (MEM_DH ** -0.5)
        p = jnp.exp(s - jnp.max(s, axis=-1, keepdims=True))
        pv = _dot_nt(p.astype(BF16), vt) / jnp.sum(p, axis=-1, keepdims=True)
        acc = acc + jnp.where(sel, pv, 0.0)
    o_ref[0] = acc


def _mem_attn(q, mkt, mvt, layer, tq):
    b, t, _ = q.shape
    m = mkt.shape[2]
    base = layer * b
    return pl.pallas_call(
        _mem_attn_kernel,
        out_shape=jax.ShapeDtypeStruct((b, t, MEM_W), F32),
        grid=(b, t // tq),
        in_specs=[pl.BlockSpec((1, tq, MEM_W), lambda i, j: (i, j, 0)),
                  pl.BlockSpec((1, MEM_W, m), lambda i, j: (base + i, 0, 0)),
                  pl.BlockSpec((1, MEM_W, m), lambda i, j: (base + i, 0, 0))],
        out_specs=pl.BlockSpec((1, tq, MEM_W), lambda i, j: (i, j, 0)),
        compiler_params=_params(("parallel", "parallel")),
        name="mem_attention",
    )(q, mkt, mvt)


def _a_out_kernel(x_ref, o_ref, r_ref, go_ref, om_ref, wa_ref, wm_ref, y_ref):
    o = o_ref[...]
    parts = []
    for h in range(GLA_HEADS):
        oh = o[:, h * DV_PAD:(h + 1) * DV_PAD]
        ms = jnp.sum(oh * oh, axis=-1, keepdims=True) * (1.0 / GLA_DV)
        parts.append(oh * lax.rsqrt(ms + EPS))
    on = jnp.concatenate(parts, axis=-1) * go_ref[...]
    main = (on * r_ref[...]).astype(BF16)
    y_ref[...] = (x_ref[...] + _dot(main, wa_ref[...])
                  + _dot(om_ref[...].astype(BF16), wm_ref[...]))


def _a_out(x, o, r, go, om, wa, wm, tm):
    n = x.shape[0]
    row = lambda c: pl.BlockSpec((tm, c), lambda i: (i, 0))
    full = lambda a: pl.BlockSpec(a.shape, lambda i: (0,) * a.ndim)
    return pl.pallas_call(
        _a_out_kernel,
        out_shape=jax.ShapeDtypeStruct((n, D_MODEL), F32),
        grid=(n // tm,),
        in_specs=[row(D_MODEL), row(o.shape[1]), row(r.shape[1]), full(go), row(MEM_W),
                  full(wa), full(wm)],
        out_specs=row(D_MODEL),
        compiler_params=_params(("parallel",)),
        name="layer_a_out",
    )(x, o, r, go, om, wa, wm)


def _b_out_kernel(x_ref, o_ref, om_ref, wa_ref, wm_ref, y_ref):
    y_ref[...] = (x_ref[...] + _dot(o_ref[...].astype(BF16), wa_ref[...])
                  + _dot(om_ref[...].astype(BF16), wm_ref[...]))


def _b_out(x, o, om, wa, wm, tm):
    n = x.shape[0]
    row = lambda c: pl.BlockSpec((tm, c), lambda i: (i, 0))
    full = lambda a: pl.BlockSpec(a.shape, lambda i: (0,) * a.ndim)
    return pl.pallas_call(
        _b_out_kernel,
        out_shape=jax.ShapeDtypeStruct((n, D_MODEL), F32),
        grid=(n // tm,),
        in_specs=[row(D_MODEL), row(o.shape[1]), row(MEM_W), full(wa), full(wm)],
        out_specs=row(D_MODEL),
        compiler_params=_params(("parallel",)),
        name="layer_b_out",
    )(x, o, om, wa, wm)


L_C0 = 0
L_A0 = MLA_KV_RANK
L_B0 = L_A0 + LANE
L_COLS = L_B0 + LANE
KV_W = MLA_HEADS * HEAD_PAD


def _latent_kernel(x_ref, g_ref, wl_ref, gc_ref, ga_ref, gb_ref, cos_ref, sin_ref,
                   wuk_ref, gkn_ref, wuvt_ref, ones_ref,
                   c_ref, kpe_ref, kf_ref, vt_ref):
    hn = _rms(x_ref[...], g_ref[...]).astype(BF16)
    y = _dot(hn, wl_ref[...])
    c = _rms(y[:, L_C0:L_A0], gc_ref[...])
    c_ref[...] = c
    a = y[:, L_A0:L_B0]
    b = y[:, L_B0:L_COLS]
    r = lax.rsqrt(jnp.sum(a * a, axis=-1, keepdims=True) * (1.0 / MLA_DR) + EPS)
    kpe = a * r * ga_ref[...] * cos_ref[...] + b * r * gb_ref[...] * sin_ref[...]
    kpe_ref[...] = kpe
    cb = c.astype(BF16)
    kn = _dot(cb, wuk_ref[...])
    ones = ones_ref[...]
    gkn = gkn_ref[...]
    for h in range(MLA_HEADS):
        hs = slice(h * HEAD_PAD, (h + 1) * HEAD_PAD)
        knh = kn[:, hs]
        ms = _group_sum(knh * knh, ones) * (1.0 / MLA_DN)
        kf_ref[:, hs] = (knh * lax.rsqrt(ms + EPS) * gkn + kpe).astype(BF16)
    vt = _dot_nt(wuvt_ref[...], cb)
    slab_row = lax.broadcasted_iota(jnp.int32, vt.shape, 0) % HEAD_PAD
    vt_ref[...] = jnp.where(slab_row == V_ONE, 1.0, vt).astype(BF16)


def _latent(x, g, wl, gc, ga, gb, cos, sin, wuk, gkn, wuvt, ones_h, tm):
    n = x.shape[0]
    row = lambda c: pl.BlockSpec((tm, c), lambda i: (i, 0))
    full = lambda a: pl.BlockSpec(a.shape, lambda i: (0,) * a.ndim)
    return pl.pallas_call(
        _latent_kernel,
        out_shape=(jax.ShapeDtypeStruct((n, MLA_KV_RANK), F32), jax.ShapeDtypeStruct((n, LANE), F32),
                   jax.ShapeDtypeStruct((n, KV_W), BF16), jax.ShapeDtypeStruct((KV_W, n), BF16)),
        grid=(n // tm,),
        in_specs=[row(D_MODEL), full(g), full(wl), full(gc), full(ga), full(gb), row(LANE), row(LANE),
                  full(wuk), full(gkn), full(wuvt), full(ones_h)],
        out_specs=(row(MLA_KV_RANK), row(LANE), row(KV_W), pl.BlockSpec((KV_W, tm), lambda i: (0, i))),
        compiler_params=_params(("parallel",)),
        name="shared_latent",
    )(x, g, wl, gc, ga, gb, cos, sin, wuk, gkn, wuvt, ones_h)


B_COLS = MLA_Q_RANK + MEM_W


def _b_in_kernel(x_ref, g_ref, w_ref, gqa_ref, w1_ref, w2_ref, g1_ref, g2_ref, inv_ref,
                 cos_ref, sin_ref, gq_ref, onesh_ref, ones64_ref, q_ref, mq_ref):
    u = _rms(x_ref[...], g_ref[...]).astype(BF16)
    p = _dot(u, w_ref[...])
    cq = _rms(p[:, :MLA_Q_RANK], gqa_ref[...]).astype(BF16)
    q1 = _dot(cq, w1_ref[...])
    q2 = _dot(cq, w2_ref[...])
    onesh = onesh_ref[...]
    inv = inv_ref[...]
    c1 = g1_ref[...] * cos_ref[...]
    c2 = g2_ref[...] * sin_ref[...]
    for h in range(MLA_HEADS):
        hs = slice(h * HEAD_PAD, (h + 1) * HEAD_PAD)
        q1h = q1[:, hs]
        rs = lax.rsqrt(_group_sum(q1h * q1h, onesh) * inv + EPS)
        q_ref[:, hs] = ((q1h * c1 + q2[:, hs] * c2) * rs).astype(BF16)
    mq_ref[...] = _head_rms(p[:, MLA_Q_RANK:B_COLS], ones64_ref[...], MEM_DH, gq_ref[...])


def _b_in(x, g, w, gqa, w1, w2, g1, g2, inv, cos, sin, gq, ones_h, ones64, tm):
    n = x.shape[0]
    row = lambda c: pl.BlockSpec((tm, c), lambda i: (i, 0))
    full = lambda a: pl.BlockSpec(a.shape, lambda i: (0,) * a.ndim)
    return pl.pallas_call(
        _b_in_kernel,
        out_shape=(jax.ShapeDtypeStruct((n, KV_W), BF16), jax.ShapeDtypeStruct((n, MEM_W), F32)),
        grid=(n // tm,),
        in_specs=[row(D_MODEL), full(g), full(w), full(gqa), full(w1), full(w2), full(g1), full(g2),
                  full(inv), row(LANE), row(LANE), full(gq), full(ones_h), full(ones64)],
        out_specs=(row(KV_W), row(MEM_W)),
        compiler_params=_params(("parallel",)),
        name="layer_b_in",
    )(x, g, w, gqa, w1, w2, g1, g2, inv, cos, sin, gq, ones_h, ones64)


def _flash_kernel(q_ref, k_ref, vt_ref, o_ref, sa_sc, sb_sc, m_sc, acc_sc):
    qi = pl.program_id(2)
    tq = q_ref.shape[1]
    q = q_ref[0]

    def scores_t(kb):
        start = pl.multiple_of(kb * tq, tq)
        return _dot_nt(k_ref[0, pl.ds(start, tq), :], q)

    def accumulate(s, kb, diagonal):
        if diagonal:
            key = lax.broadcasted_iota(jnp.int32, (tq, tq), 0)
            qry = lax.broadcasted_iota(jnp.int32, (tq, tq), 1)
            s = jnp.where(key <= qry, s, NEG)
        m = m_sc[...]
        m_new = jnp.maximum(m, jnp.max(s, axis=0, keepdims=True))
        p = jnp.exp2(s - m_new).astype(BF16)
        start = pl.multiple_of(kb * tq, tq)
        acc_sc[...] = jnp.exp2(m - m_new) * acc_sc[...] + _dot(vt_ref[:, pl.ds(start, tq)], p)
        m_sc[...] = m_new

    def finish():
        acc = acc_sc[...]
        o_ref[0] = (acc / acc[V_ONE:V_ONE + 1, :]).T.astype(o_ref.dtype)

    m_sc[...] = jnp.full_like(m_sc, NEG)
    acc_sc[...] = jnp.zeros_like(acc_sc)
    sa_sc[...] = scores_t(0)

    def pair(i, carry):
        kb = 2 * i
        sb_sc[...] = scores_t(kb + 1)
        accumulate(sa_sc[...], kb, False)
        sa_sc[...] = scores_t(kb + 2)
        accumulate(sb_sc[...], kb + 1, False)
        return carry

    lax.fori_loop(0, qi // 2, pair, 0)

    @pl.when(qi % 2 == 1)
    def _():
        sb_sc[...] = scores_t(qi)
        accumulate(sa_sc[...], qi - 1, False)
        accumulate(sb_sc[...], qi, True)
        finish()

    @pl.when(qi % 2 == 0)
    def _():
        accumulate(sa_sc[...], qi, True)
        finish()


def _flash(q, k, vt, tq):
    b, t, _ = q.shape
    qspec = pl.BlockSpec((1, tq, HEAD_PAD), lambda bi, h, i: (bi, i, h))
    return pl.pallas_call(
        _flash_kernel,
        out_shape=jax.ShapeDtypeStruct((b, t, KV_W), BF16),
        grid=(b, MLA_HEADS, t // tq),
        in_specs=[qspec,
                  pl.BlockSpec((1, t, HEAD_PAD), lambda bi, h, i: (bi, 0, h)),
                  pl.BlockSpec((HEAD_PAD, t), lambda bi, h, i: (h, bi))],
        out_specs=qspec,
        scratch_shapes=[pltpu.VMEM((tq, tq), F32), pltpu.VMEM((tq, tq), F32),
                        pltpu.VMEM((1, tq), F32), pltpu.VMEM((HEAD_PAD, tq), F32)],
        compiler_params=_params(("parallel", "parallel", "arbitrary")),
        name="prompt_latent_attention",
    )(q, k, vt)


PAGES_PER_STEP = 32
QROWS = MLA_HEADS * T_PAD
SUB = 512


def _absorb_kernel(q_ref, w_ref, o_ref):
    o_ref[0] = _dot(q_ref[...], w_ref[0]).astype(BF16)


def _absorb(q, wabs):
    n = q.shape[0]
    return pl.pallas_call(
        _absorb_kernel,
        out_shape=jax.ShapeDtypeStruct((MLA_HEADS, n, MLA_KV_RANK), BF16),
        grid=(MLA_HEADS,),
        in_specs=[pl.BlockSpec((n, HEAD_PAD), lambda h: (0, h)),
                  pl.BlockSpec((1, HEAD_PAD, MLA_KV_RANK), lambda h: (h, 0, 0))],
        out_specs=pl.BlockSpec((1, n, MLA_KV_RANK), lambda h: (h, 0, 0)),
        compiler_params=_params(("parallel",)),
        name="absorb_w_uk",
    )(q, wabs)


def _scores(kt, sn, sp):
    rows = []
    for h in range(MLA_HEADS):
        kth = kt[h * MLA_DN:(h + 1) * MLA_DN, :]
        r = lax.rsqrt(jnp.sum(kth * kth, axis=0, keepdims=True) * (1.0 / MLA_DN) + EPS)
        rows.append(sn[h * T_PAD:(h + 1) * T_PAD, :] * r)
    return jnp.concatenate(rows, axis=0) + sp


def _online_softmax(s, cb, m, l, acc):
    m_new = jnp.maximum(m, jnp.max(s, axis=-1, keepdims=True))
    alpha = jnp.exp2(m - m_new)
    p = jnp.exp2(s - m_new)
    l = alpha * l + jnp.sum(p, axis=-1, keepdims=True)
    acc = alpha * acc + _dot(p.astype(BF16), cb)
    return m_new, l, acc


def _sample_attn_kernel(pt_ref, ckv_hbm, kpe_hbm, qa_ref, qp_ref, wukt_ref, cn_ref, kn_ref, o_ref,
                        cbuf, kbuf, csem, ksem, kt_sc, sn_sc, sp_sc, cb_sc, *, n_chunks, ch_pages, sub):
    b = pl.program_id(0)
    page = cbuf.shape[1] // ch_pages
    n_sub = ch_pages * page // sub
    qa = qa_ref[0]
    qp = qp_ref[0]
    wukt = wukt_ref[...]

    def chunk_copies(seq, ch, slot):
        copies = []
        for i in range(ch_pages):
            pg = pt_ref[seq, ch * ch_pages + i]
            rows = pl.ds(i * page, page)
            copies.append(pltpu.make_async_copy(ckv_hbm.at[pg], cbuf.at[slot, rows, :], csem.at[slot]))
            copies.append(pltpu.make_async_copy(kpe_hbm.at[pg], kbuf.at[slot, :, rows], ksem.at[slot]))
        return copies

    def stage1(slot, i, buf):
        cb = cbuf[slot, i * sub:(i + 1) * sub, :].astype(BF16)
        cb_sc[buf] = cb
        kt_sc[buf] = _dot_nt(wukt, cb)
        sn_sc[buf] = _dot_nt(qa, cb)
        sp_sc[buf] = _dot(qp, kbuf[slot, :, i * sub:(i + 1) * sub].astype(BF16))

    def stage2(buf, state):
        return _online_softmax(_scores(kt_sc[buf], sn_sc[buf], sp_sc[buf]), cb_sc[buf], *state)

    for c in chunk_copies(b, 0, 0):
        c.start()

    cn = cn_ref[0].astype(BF16)
    s = _scores(_dot_nt(wukt, cn), _dot_nt(qa, cn), _dot(qp, kn_ref[0].astype(BF16)))
    t_q = lax.broadcasted_iota(jnp.int32, s.shape, 0) % T_PAD
    t_k = lax.broadcasted_iota(jnp.int32, s.shape, 1)
    init = (jnp.full((QROWS, 1), NEG, F32), jnp.zeros((QROWS, 1), F32), jnp.zeros((QROWS, MLA_KV_RANK), F32))
    state = _online_softmax(jnp.where(t_k <= t_q, s, NEG), cn, *init)

    def chunk(ch, slot, state):
        for c in chunk_copies(b, ch, slot):
            c.wait()
        for c in chunk_copies(b, (ch + 1) % n_chunks, 1 - slot):
            c.start()
        stage1(slot, 0, 0)
        for i in range(n_sub):
            if i + 1 < n_sub:
                stage1(slot, i + 1, (i + 1) % 2)
            state = stage2(i % 2, state)
        return state

    def chunk_pair(pi, state):
        return chunk(2 * pi + 1, 1, chunk(2 * pi, 0, state))

    _, l, acc = lax.fori_loop(0, n_chunks // 2, chunk_pair, state)
    for c in chunk_copies(b, 0, 0):
        c.wait()
    o_ref[0] = acc / l


def _sample_attn(page_table, cache_ckv, cache_kpe_t, qa, qp, wukt, c_new, k_new_t):
    nb, n_pages = page_table.shape
    page = cache_ckv.shape[1]
    ch_pages = math.gcd(n_pages // 2, PAGES_PER_STEP)
    n_chunks = n_pages // ch_pages
    tokens = ch_pages * page
    sub = min(SUB, tokens // 2)
    assert tokens % (2 * sub) == 0 and n_chunks % 2 == 0
    per_b = lambda r, w: pl.BlockSpec((1, r, w), lambda b, pt: (b, 0, 0))
    hbm = pl.BlockSpec(memory_space=pl.ANY)
    return pl.pallas_call(
        functools.partial(_sample_attn_kernel, n_chunks=n_chunks, ch_pages=ch_pages, sub=sub),
        out_shape=jax.ShapeDtypeStruct((nb, QROWS, MLA_KV_RANK), F32),
        grid_spec=pltpu.PrefetchScalarGridSpec(
            num_scalar_prefetch=1,
            grid=(nb,),
            in_specs=[hbm, hbm, per_b(QROWS, MLA_KV_RANK), per_b(QROWS, MLA_DR),
                      pl.BlockSpec(wukt.shape, lambda b, pt: (0, 0)),
                      per_b(c_new.shape[1], MLA_KV_RANK), per_b(MLA_DR, k_new_t.shape[2])],
            out_specs=per_b(QROWS, MLA_KV_RANK),
            scratch_shapes=[pltpu.VMEM((2, tokens, MLA_KV_RANK), F32),
                            pltpu.VMEM((2, MLA_DR, tokens), F32),
                            pltpu.SemaphoreType.DMA((2,)), pltpu.SemaphoreType.DMA((2,)),
                            pltpu.VMEM((2, MLA_HEADS * MLA_DN, sub), F32),
                            pltpu.VMEM((2, QROWS, sub), F32),
                            pltpu.VMEM((2, QROWS, sub), F32),
                            pltpu.VMEM((2, sub, MLA_KV_RANK), BF16)],
        ),
        compiler_params=_params(("arbitrary",)),
        name="sample_latent_attention",
    )(page_table, cache_ckv, cache_kpe_t, qa, qp, wukt, c_new, k_new_t)


def _uv_kernel(x_ref, w_ref, o_ref):
    o_ref[0] = _dot(x_ref[0].astype(BF16), w_ref[0])


def _uv_project(ctx, wuv):
    _, n, _ = ctx.shape
    return pl.pallas_call(
        _uv_kernel,
        out_shape=jax.ShapeDtypeStruct((MLA_HEADS, n, HEAD_PAD), F32),
        grid=(MLA_HEADS,),
        in_specs=[pl.BlockSpec((1, n, MLA_KV_RANK), lambda h: (h, 0, 0)),
                  pl.BlockSpec((1, MLA_KV_RANK, HEAD_PAD), lambda h: (h, 0, 0))],
        out_specs=pl.BlockSpec((1, n, HEAD_PAD), lambda h: (h, 0, 0)),
        compiler_params=_params(("parallel",)),
        name="apply_w_uv",
    )(ctx, wuv)


def _memkv_kernel(m_ref, g_ref, wkt_ref, wvt_ref, gk_ref, ones_ref, kt_ref, vt_ref):
    mn = _rms(m_ref[0], g_ref[0]).astype(BF16)
    kt = _dot_nt(wkt_ref[0], mn)
    hi, mid, _ = _split3(kt * kt)
    ones = ones_ref[...]
    ms = (_dot(ones, hi) + _dot(ones, mid)) * (1.0 / MEM_DH)
    kt_ref[0, 0] = kt * lax.rsqrt(ms + EPS) * gk_ref[0]
    vt_ref[0, 0] = _dot_nt(wvt_ref[0], mn)


def _memkv(mem, g, wkt, wvt, gk, ones64):
    nb, m, _ = mem.shape
    nl = wkt.shape[0]
    lay = lambda a: pl.BlockSpec((1,) + a.shape[1:], lambda l, b: (l,) + (0,) * (a.ndim - 1))
    out = pl.BlockSpec((1, 1, MEM_W, m), lambda l, b: (l, b, 0, 0))
    shape = jax.ShapeDtypeStruct((nl, nb, MEM_W, m), F32)
    return pl.pallas_call(
        _memkv_kernel,
        out_shape=(shape, shape),
        grid=(nl, nb),
        in_specs=[pl.BlockSpec((1, m, D_MODEL), lambda l, b: (b, 0, 0)), lay(g), lay(wkt), lay(wvt), lay(gk),
                  pl.BlockSpec(ones64.shape, lambda l, b: (0, 0))],
        out_specs=(out, out),
        compiler_params=_params(("parallel", "parallel")),
        name="prompt_mem_kv",
    )(mem, g, wkt, wvt, gk, ones64)


def _pad_heads(w, heads, size, padded, axis=-1):
    axis = axis % w.ndim
    shape = w.shape[:axis] + (heads, size) + w.shape[axis + 1:]
    w = w.reshape(shape)
    pad = [(0, 0)] * w.ndim
    pad[axis + 1] = (0, padded - size)
    w = jnp.pad(w, pad)
    return w.reshape(w.shape[:axis] + (heads * padded,) + w.shape[axis + 2:])


def _block_ones(n, sizes):
    gid = []
    g = 0
    while len(gid) < n:
        for s in sizes:
            gid += [g] * s
            g += 1
    gid = jnp.asarray(gid[:n], jnp.int32)
    return (gid[:, None] == gid[None, :]).astype(BF16)


def _rope_tables(pos):
    half = MLA_DR // 2
    inv = ROPE_THETA ** (-jnp.arange(half, dtype=F32) / half)
    ang = pos.astype(F32)[:, None] * inv
    cos, sin = jnp.cos(ang), jnp.sin(ang)
    z = jnp.zeros((pos.shape[0], MLA_DN), F32)
    z2 = jnp.zeros((pos.shape[0], HEAD_PAD - MLA_DN - MLA_DR), F32)
    return (jnp.concatenate([z + 1.0, cos, cos, z2], axis=-1), jnp.concatenate([z, -sin, sin, z2], axis=-1))


def _swap_halves(w):
    half = w.shape[-1] // 2
    return jnp.concatenate([w[..., half:], w[..., :half]], axis=-1)


def _rope_slab(w):
    pad = [(0, 0)] * (w.ndim - 1) + [(MLA_DN, HEAD_PAD - MLA_DN - MLA_DR)]
    return jnp.pad(w, pad)


def _tile_for(n, pref):
    t = min(n, pref)
    while n % t:
        t //= 2
    return t


def kernel(x_prompt, x_sample, mem_prompt, cache_mem_k, cache_mem_v, state_gla, cache_ckv, cache_kpe, page_table,
           ffn1_norm, ffn1_w_gate, ffn1_w_up, ffn1_w_down, ffn2_norm, ffn2_w_gate, ffn2_w_up, ffn2_w_down,
           mix_norm, w_out, mem_norm, w_mem_k, w_mem_v, mem_k_norm, mem_q_norm,
           gla_w_in, gla_w_gate2, gla_b_gate, gla_o_norm,
           mla_w_in, mla_q_a_norm, mla_w_uq, mla_q_nope_norm, mla_q_pe_norm,
           kv_norm, kv_w_dkv, kv_ckv_norm, kv_w_kr, kv_kpe_norm, kv_w_uk, kv_k_nope_norm, kv_w_uv):
    bp, tp, _ = x_prompt.shape
    bs, ts, _ = x_sample.shape
    n_mem = mem_prompt.shape[1]
    past = page_table.shape[1] * cache_ckv.shape[1]
    row = lambda a: a.reshape(1, -1).astype(F32)
    bf = lambda a: a.astype(BF16)

    ffn = [[(row(n[l]), bf(g[l]), bf(u[l]), bf(d[l])) for l in range(2)]
           for n, g, u, d in ((ffn1_norm, ffn1_w_gate, ffn1_w_up, ffn1_w_down),
                              (ffn2_norm, ffn2_w_gate, ffn2_w_up, ffn2_w_down))]
    ones64 = _block_ones(MEM_W, [MEM_DH])
    ones_head = _block_ones(HEAD_PAD, [MLA_DN, MLA_DR, HEAD_PAD - MLA_DN - MLA_DR])
    gmq = [row(jnp.tile(mem_q_norm[l], MEM_HEADS)) for l in range(2)]

    wi = gla_w_in[0]
    o_q, o_k, o_v, o_g, o_r, o_m = 0, GLA_QK, 2 * GLA_QK, 2 * GLA_QK + GLA_V, 2 * GLA_QK + GLA_V + GLA_GATE_RANK, \
        2 * GLA_QK + 2 * GLA_V + GLA_GATE_RANK
    wa_in = bf(jnp.concatenate([
        _pad_heads(wi[:, o_q:o_k], GLA_HEADS, GLA_DK, DK_PAD),
        _pad_heads(wi[:, o_k:o_v], GLA_HEADS, GLA_DK, DK_PAD),
        _pad_heads(wi[:, o_v:o_g], GLA_HEADS, GLA_DV, DV_PAD),
        _pad_heads(wi[:, o_r:o_m], GLA_HEADS, GLA_DV, DV_PAD),
        wi[:, o_m:],
        jnp.pad(wi[:, o_g:o_r], ((0, 0), (0, LANE - GLA_GATE_RANK))),
    ], axis=1))
    w_gate2 = bf(jnp.pad(_pad_heads(gla_w_gate2[0], GLA_HEADS, GLA_DK, DK_PAD),
                         ((0, LANE - GLA_GATE_RANK), (0, 0))))
    b_gate = row(_pad_heads(gla_b_gate[0], GLA_HEADS, GLA_DK, DK_PAD))
    g_o = row(jnp.pad(gla_o_norm[0], (0, DV_PAD - GLA_DV)))
    g_o = jnp.tile(g_o, (1, GLA_HEADS))
    wa_out_main = bf(_pad_heads(w_out[0][:GLA_V], GLA_HEADS, GLA_DV, DV_PAD, axis=0))
    wa_out_mem = bf(w_out[0][GLA_V:])

    kr_slab = _rope_slab(kv_w_kr)
    w_lat = bf(jnp.concatenate([kv_w_dkv, kr_slab, _rope_slab(_swap_halves(kv_w_kr))], axis=1))
    g_kpe_a = row(_rope_slab(kv_kpe_norm))
    g_kpe_b = row(_rope_slab(_swap_halves(kv_kpe_norm)))
    w_uk_pad = bf(_pad_heads(kv_w_uk, MLA_HEADS, MLA_DN, HEAD_PAD))
    w_uv_t = bf(_pad_heads(kv_w_uv, MLA_HEADS, MLA_DV, HEAD_PAD).T)
    g_kn_slab = row(jnp.pad(kv_k_nope_norm, (0, HEAD_PAD - MLA_DN)))

    scale = (MLA_DN + MLA_DR) ** -0.5 * LOG2E
    wuq = mla_w_uq[0].reshape(MLA_Q_RANK, MLA_HEADS, MLA_DN + MLA_DR)
    w_q1 = bf(jnp.pad(wuq, ((0, 0), (0, 0), (0, HEAD_PAD - MLA_DN - MLA_DR))).reshape(MLA_Q_RANK, KV_W))
    w_q2 = bf(_rope_slab(_swap_halves(wuq[..., MLA_DN:])).reshape(MLA_Q_RANK, KV_W))
    g_q1 = row(jnp.concatenate([mla_q_nope_norm[0], mla_q_pe_norm[0],
                                jnp.zeros((HEAD_PAD - MLA_DN - MLA_DR,), F32)])) * scale
    g_q2 = row(_rope_slab(_swap_halves(mla_q_pe_norm[0]))) * scale
    inv_sizes = row(jnp.concatenate([jnp.full((MLA_DN,), 1.0 / MLA_DN, F32),
                                     jnp.full((HEAD_PAD - MLA_DN,), 1.0 / MLA_DR, F32)]))
    wb_in = bf(mla_w_in[0])
    wb_out_main = bf(_pad_heads(w_out[1][:MLA_HEADS * MLA_DV], MLA_HEADS, MLA_DV, HEAD_PAD, axis=0))
    wb_out_mem = bf(w_out[1][MLA_HEADS * MLA_DV:])
    wuk_h = kv_w_uk.reshape(MLA_KV_RANK, MLA_HEADS, MLA_DN).transpose(1, 2, 0)
    w_abs = bf(jnp.pad(wuk_h * kv_k_nope_norm[None, :, None], ((0, 0), (0, HEAD_PAD - MLA_DN), (0, 0))))
    w_uk_t = bf(kv_w_uk.T)
    w_uv_h = bf(jnp.pad(kv_w_uv.reshape(MLA_KV_RANK, MLA_HEADS, MLA_DV).transpose(1, 0, 2),
                        ((0, 0), (0, 0), (0, HEAD_PAD - MLA_DV))))

    g_mem_k = jnp.broadcast_to(jnp.tile(mem_k_norm, (1, MEM_HEADS))[:, :, None], (2, MEM_W, n_mem))
    mem_kt_p, mem_vt_p = _memkv(mem_prompt, mem_norm.reshape(2, 1, D_MODEL),
                                bf(w_mem_k.swapaxes(1, 2)), bf(w_mem_v.swapaxes(1, 2)), g_mem_k, ones64)

    def mem_out(a):
        return a.reshape(2, bp, MEM_HEADS, MEM_DH, n_mem).transpose(0, 1, 4, 2, 3)

    def trunk(x, nb, t, pos, mem_k, mem_v, s0, attend):
        n = nb * t
        tm = _tile_for(n, FFN_ROWS)
        tp_ = _tile_for(n, PROJ_ROWS)
        ta = _tile_for(n, A_IN_ROWS)
        seq_pad = (-t) % T_PAD
        tpad = t + seq_pad

        def to_seq(a):
            a = a.reshape(nb, t, a.shape[-1])
            return jnp.pad(a, ((0, 0), (0, seq_pad), (0, 0))) if seq_pad else a

        def from_seq(a):
            return a[:, :t].reshape(n, a.shape[-1])

        cos, sin = _rope_tables(pos)
        cos = jnp.tile(cos, (nb, 1))
        sin = jnp.tile(sin, (nb, 1))

        def mem_part(mq, l):
            return from_seq(_mem_attn(to_seq(mq), mem_k, mem_v, l, _tile_for(tpad, 1024)))

        x = _ffn_half(x, *ffn[0][0], tm)
        q, k, v, la, r, mq = _a_in(x, row(mix_norm[0]), wa_in, w_gate2, b_gate, gmq[0], ones64, ta)
        s0t = jnp.pad(s0.transpose(0, 1, 3, 2), ((0, 0), (0, 0), (0, DV_PAD - GLA_DV), (0, DK_PAD - GLA_DK)))
        o, st = _gla(to_seq(q), to_seq(k), to_seq(v), to_seq(la), s0t, math.gcd(tpad, GLA_CHUNK))
        gla_state = st[:, :, :GLA_DV, :GLA_DK].transpose(0, 1, 3, 2)
        x = _a_out(x, from_seq(o), r, g_o, mem_part(mq, 0), wa_out_main, wa_out_mem, tp_)
        x = _ffn_half(x, *ffn[1][0], tm)
        c, kpe, k_full, v_full = _latent(x, row(kv_norm), w_lat, row(kv_ckv_norm), g_kpe_a, g_kpe_b, cos, sin,
                                         w_uk_pad, g_kn_slab, w_uv_t, ones_head, tp_)
        kpe = kpe[:, MLA_DN:MLA_DN + MLA_DR]
        x = _ffn_half(x, *ffn[0][1], tm)
        q_full, mq = _b_in(x, row(mix_norm[1]), wb_in, row(mla_q_a_norm[0]), w_q1, w_q2, g_q1, g_q2, inv_sizes,
                           cos, sin, gmq[1], ones_head, ones64, tp_)
        o_main = attend(q_full, k_full, v_full, c, kpe)
        x = _b_out(x, o_main, mem_part(mq, 1), wb_out_main, wb_out_mem, tp_)
        x = _ffn_half(x, *ffn[1][1], tm)
        return x, gla_state, c, kpe

    def attend_prompt(q_full, k_full, v_full, c, kpe):
        sh = (bp, tp, KV_W)
        o = _flash(q_full.reshape(sh), k_full.reshape(sh), v_full, _tile_for(tp, 512))
        return o.reshape(bp * tp, KV_W)

    def attend_sample(q_full, k_full, v_full, c, kpe):
        n = bs * ts
        qa = _absorb(q_full, w_abs)

        def rows(a):
            a = a.reshape(MLA_HEADS, bs, ts, a.shape[-1]).transpose(1, 0, 2, 3)
            a = jnp.pad(a, ((0, 0), (0, 0), (0, T_PAD - ts), (0, 0)))
            return a.reshape(bs, QROWS, a.shape[-1])

        qp = q_full.reshape(n, MLA_HEADS, HEAD_PAD)[:, :, MLA_DN:MLA_DN + MLA_DR].transpose(1, 0, 2)
        page = cache_ckv.shape[1]
        c_new = jnp.pad(c.reshape(bs, ts, MLA_KV_RANK), ((0, 0), (0, page - ts), (0, 0)))
        k_new_t = jnp.pad(kpe.reshape(bs, ts, MLA_DR), ((0, 0), (0, page - ts), (0, 0))).swapaxes(1, 2)
        ctx = _sample_attn(page_table, cache_ckv, cache_kpe.swapaxes(1, 2), rows(qa), rows(qp), w_uk_t,
                           c_new, k_new_t)
        ctx = ctx.reshape(bs, MLA_HEADS, T_PAD, MLA_KV_RANK).transpose(1, 0, 2, 3)
        o = _uv_project(ctx.reshape(MLA_HEADS, bs * T_PAD, MLA_KV_RANK), w_uv_h)
        o = o.reshape(MLA_HEADS, bs, T_PAD, HEAD_PAD)[:, :, :ts].transpose(1, 2, 0, 3)
        return o.reshape(n, KV_W)

    mk_p = mem_kt_p.reshape(2 * bp, MEM_W, n_mem)
    mv_p = mem_vt_p.reshape(2 * bp, MEM_W, n_mem)
    s0_p = jnp.zeros((bp, GLA_HEADS, GLA_DK, GLA_DV), F32)
    y_p, st_p, c_p, kpe_p = trunk(x_prompt.reshape(bp * tp, D_MODEL), bp, tp, jnp.arange(tp), mk_p, mv_p,
                                  s0_p, attend_prompt)

    def mem_in(a):
        return a.transpose(0, 1, 3, 4, 2).reshape(2 * bs, MEM_W, n_mem)

    mk_s = mem_in(cache_mem_k)
    mv_s = mem_in(cache_mem_v)
    y_s, st_s, c_s, kpe_s = trunk(x_sample.reshape(bs * ts, D_MODEL), bs, ts, past + jnp.arange(ts), mk_s, mv_s,
                                  state_gla[0], attend_sample)

    return (y_p.reshape(bp, tp, D_MODEL), y_s.reshape(bs, ts, D_MODEL),
            st_p[None], st_s[None],
            c_p.reshape(bp, tp, MLA_KV_RANK), kpe_p.reshape(bp, tp, MLA_DR),
            c_s.reshape(bs, ts, MLA_KV_RANK), kpe_s.reshape(bs, ts, MLA_DR),
            mem_out(mem_kt_p), mem_out(mem_vt_p))
```

```python
import functools
import math

import jax
import jax.numpy as jnp
from jax import lax
from jax.experimental import pallas as pl
from jax.experimental.pallas import tpu as pltpu

F32 = jnp.float32
BF16 = jnp.bfloat16

D_MODEL = 1024
D_FF = 2816
GLA_HEADS = 4
GLA_DK = 96
GLA_DV = 192
GLA_QK = GLA_HEADS * GLA_DK
GLA_V = GLA_HEADS * GLA_DV
GLA_GATE_RANK = 16
GLA_GATE_TEMP = 16.0
MEM_HEADS = 4
MEM_DH = 64
MEM_W = MEM_HEADS * MEM_DH
MLA_HEADS = 12
MLA_DN = 64
MLA_DR = 32
MLA_DV = 64
MLA_KV_RANK = 256
MLA_Q_RANK = 384
ROPE_THETA = 10000.0
EPS = 1e-6

LANE = 128
DK_PAD = LANE
DV_PAD = 2 * LANE
HEAD_PAD = LANE
V_ONE = MLA_DV
LOG2E = math.log2(math.e)
GLA_CHUNK = 64
GLA_SUB = 8
GLA_SEQS = 2
T_PAD = 8
NEG = -1e30
VMEM_LIMIT = 56 * 1024 * 1024
FFN_ROWS = 1024
FLASH_ROWS = 1024
FFN_COLS = 256
PROJ_ROWS = 512
A_IN_ROWS = 256

NT_DIMS = (((1,), (1,)), ((), ()))
TN_DIMS = (((0,), (0,)), ((), ()))


def _dot(a, b):
    return jnp.dot(a, b, preferred_element_type=F32)


def _dot_nt(a, b):
    return lax.dot_general(a, b, NT_DIMS, preferred_element_type=F32)


def _dot_tn(a, b):
    return lax.dot_general(a, b, TN_DIMS, preferred_element_type=F32)


def _rms(x, g):
    return x * lax.rsqrt(jnp.mean(x * x, axis=-1, keepdims=True) + EPS) * g


def _split3(x):
    hi = x.astype(BF16)
    r1 = x - hi.astype(F32)
    mid = r1.astype(BF16)
    lo = (r1 - mid.astype(F32)).astype(BF16)
    return hi, mid, lo


def _group_sum(x, ones_bd):
    hi, mid, _ = _split3(x)
    return _dot(hi, ones_bd) + _dot(mid, ones_bd)


def _params(sem):
    return pltpu.CompilerParams(dimension_semantics=sem, vmem_limit_bytes=VMEM_LIMIT)


def _ffn_kernel(x_ref, g_ref, wg_ref, wu_ref, wd_ref, o_ref, act_sc):
    x = x_ref[...]
    h = _rms(x, g_ref[...]).astype(BF16)
    for f in range(D_FF // FFN_COLS):
        cols = slice(f * FFN_COLS, (f + 1) * FFN_COLS)
        gate = _dot(h, wg_ref[:, cols])
        up = _dot(h, wu_ref[:, cols])
        act_sc[:, cols] = (gate * jax.nn.sigmoid(gate) * up).astype(BF16)
    o_ref[...] = x + 0.5 * _dot(act_sc[...], wd_ref[...])


def _ffn_half(x, g, wg, wu, wd, tm):
    n = x.shape[0]
    resident = lambda a: pl.BlockSpec(a.shape, lambda i: (0,) * a.ndim, pipeline_mode=pl.Buffered(1))
    return pl.pallas_call(
        _ffn_kernel,
        out_shape=jax.ShapeDtypeStruct((n, D_MODEL), F32),
        grid=(n // tm,),
        in_specs=[pl.BlockSpec((tm, D_MODEL), lambda i: (i, 0)), resident(g), resident(wg), resident(wu),
                  resident(wd)],
        out_specs=pl.BlockSpec((tm, D_MODEL), lambda i: (i, 0)),
        scratch_shapes=[pltpu.VMEM((tm, D_FF), BF16)],
        compiler_params=_params(("parallel",)),
        name="ffn_half",
    )(x, g, wg, wu, wd)


A_Q0 = 0
A_K0 = A_Q0 + GLA_HEADS * DK_PAD
A_V0 = A_K0 + GLA_HEADS * DK_PAD
A_R0 = A_V0 + GLA_HEADS * DV_PAD
A_M0 = A_R0 + GLA_HEADS * DV_PAD
A_G0 = A_M0 + MEM_W
A_COLS = A_G0 + LANE


def _head_rms(x, ones_bd, size, g):
    return x * lax.rsqrt(_group_sum(x * x, ones_bd) * (1.0 / size) + EPS) * g


def _a_in_kernel(x_ref, g_ref, w_ref, w2_ref, b2_ref, gq_ref, ones_ref,
                 q_ref, k_ref, v_ref, la_ref, r_ref, mq_ref):
    u = _rms(x_ref[...], g_ref[...]).astype(BF16)
    p = _dot(u, w_ref[...])
    q_ref[...] = p[:, A_Q0:A_K0] * (GLA_DK ** -0.5)
    k_ref[...] = p[:, A_K0:A_V0]
    v_ref[...] = p[:, A_V0:A_R0]
    r = p[:, A_R0:A_M0]
    r_ref[...] = r * jax.nn.sigmoid(r)
    mq_ref[...] = _head_rms(p[:, A_M0:A_G0], ones_ref[...], MEM_DH, gq_ref[...])
    z = _dot(p[:, A_G0:A_COLS].astype(BF16), w2_ref[...]) + b2_ref[...]
    log_sig = jnp.minimum(z, 0.0) - jnp.log1p(jnp.exp(-jnp.abs(z)))
    la_ref[...] = log_sig * (1.0 / GLA_GATE_TEMP)


def _a_in(x, g, w, w2, b2, gq, ones64, tm):
    n = x.shape[0]
    qk = GLA_HEADS * DK_PAD
    vv = GLA_HEADS * DV_PAD
    row = lambda c: pl.BlockSpec((tm, c), lambda i: (i, 0))
    full = lambda a: pl.BlockSpec(a.shape, lambda i: (0,) * a.ndim)
    return pl.pallas_call(
        _a_in_kernel,
        out_shape=(jax.ShapeDtypeStruct((n, qk), F32), jax.ShapeDtypeStruct((n, qk), F32),
                   jax.ShapeDtypeStruct((n, vv), F32), jax.ShapeDtypeStruct((n, qk), F32),
                   jax.ShapeDtypeStruct((n, vv), F32), jax.ShapeDtypeStruct((n, MEM_W), F32)),
        grid=(n // tm,),
        in_specs=[row(D_MODEL), full(g), full(w), full(w2), full(b2), full(gq), full(ones64)],
        out_specs=(row(qk), row(qk), row(vv), row(qk), row(vv), row(MEM_W)),
        compiler_params=_params(("parallel",)),
        name="layer_a_in",
    )(x, g, w, w2, b2, gq, ones64)


def _gla_kernel(q_ref, k_ref, v_ref, la_ref, s0_ref, o_ref, s_out_ref, st_sc):
    c = pl.program_id(1)
    nseq, chunk, _ = q_ref.shape
    sub = min(GLA_SUB, chunk)
    n_sc = chunk // sub

    @pl.when(c == 0)
    def _():
        st_sc[...] = s0_ref[...]

    t_idx = lax.broadcasted_iota(jnp.int32, (chunk, chunk), 0)
    s_idx = lax.broadcasted_iota(jnp.int32, (chunk, chunk), 1)
    tri = jnp.where(s_idx <= t_idx, 1.0, 0.0).astype(BF16)
    later = [(s_idx // sub == j) & (t_idx // sub > j) for j in range(n_sc - 1)]
    diag = [(s_idx == t_idx - d) & (t_idx % sub >= d) for d in range(sub)]
    for g in range(nseq):
        hi, mid, lo = _split3(la_ref[g])
        a_cum = _dot(tri, hi) + _dot(tri, mid) + _dot(tri, lo)
        a_last = a_cum[chunk - 1:chunk, :]
        a_end = jnp.concatenate(
            [jnp.broadcast_to(a_cum[(j + 1) * sub - 1:(j + 1) * sub, :], (sub, a_cum.shape[1]))
             for j in range(n_sc)], axis=0)
        e_q = jnp.exp(a_cum)
        e_kend = jnp.exp(a_end - a_cum)
        e_kd = jnp.exp(a_last - a_cum)
        e_last = jnp.exp(a_last)
        q = q_ref[g]
        k = k_ref[g]
        v = v_ref[g]
        for h in range(GLA_HEADS):
            ks = slice(h * DK_PAD, (h + 1) * DK_PAD)
            vs = slice(h * DV_PAD, (h + 1) * DV_PAD)
            a = a_cum[:, ks]
            qf = q[:, ks]
            kf = k[:, ks]
            k_end = (kf * e_kend[:, ks]).astype(BF16)
            scores = jnp.zeros((chunk, chunk), F32)
            for j in range(n_sc - 1):
                e_j = a[(j + 1) * sub - 1:(j + 1) * sub, :]
                q_j = (qf * jnp.exp(jnp.minimum(a - e_j, 0.0))).astype(BF16)
                scores = scores + jnp.where(later[j], _dot_nt(q_j, k_end), 0.0)
            for d in range(sub):
                k_d = kf if d == 0 else pltpu.roll(kf, d, 0)
                a_d = a if d == 0 else pltpu.roll(a, d, 0)
                pair = jnp.sum(qf * k_d * jnp.exp(jnp.minimum(a - a_d, 0.0)), axis=-1, keepdims=True)
                scores = scores + jnp.where(diag[d], pair, 0.0)
            qh = (qf * e_q[:, ks]).astype(BF16)
            kd = (kf * e_kd[:, ks]).astype(BF16)
            vh = v[:, vs].astype(BF16)
            st = st_sc[g, h]
            o_ref[g, :, vs] = _dot(scores.astype(BF16), vh) + _dot_nt(qh, st.astype(BF16))
            st_sc[g, h] = st * e_last[:, ks] + _dot_tn(vh, kd)

    @pl.when(c == pl.num_programs(1) - 1)
    def _():
        s_out_ref[...] = st_sc[...]


def _gla(q, k, v, la, s0t, chunk):
    b, t, _ = q.shape
    qk = GLA_HEADS * DK_PAD
    vv = GLA_HEADS * DV_PAD
    nseq = math.gcd(b, GLA_SEQS)
    seq = lambda c: pl.BlockSpec((nseq, chunk, c), lambda i, j: (i, j, 0))
    st = pl.BlockSpec((nseq, GLA_HEADS, DV_PAD, DK_PAD), lambda i, j: (i, 0, 0, 0))
    return pl.pallas_call(
        _gla_kernel,
        out_shape=(jax.ShapeDtypeStruct((b, t, vv), F32),
                   jax.ShapeDtypeStruct((b, GLA_HEADS, DV_PAD, DK_PAD), F32)),
        grid=(b // nseq, t // chunk),
        in_specs=[seq(qk), seq(qk), seq(vv), seq(qk), st],
        out_specs=(seq(vv), st),
        scratch_shapes=[pltpu.VMEM((nseq, GLA_HEADS, DV_PAD, DK_PAD), F32)],
        compiler_params=_params(("parallel", "arbitrary")),
        name="gla_recurrence",
    )(q, k, v, la, s0t)


def _mem_attn_kernel(q_ref, kt_ref, vt_ref, o_ref):
    q = q_ref[0]
    kt = kt_ref[0].astype(BF16)
    vt = vt_ref[0].astype(BF16)
    head = lax.broadcasted_iota(jnp.int32, (1, MEM_W), 1) // MEM_DH
    acc = jnp.zeros(q.shape, F32)
    for h in range(MEM_HEADS):
        sel = head == h
        qh = jnp.where(sel, q, 0.0).astype(BF16)
        s = _dot(qh, kt) * (MEM_DH ** -0.5)
        p = jnp.exp(s - jnp.max(s, axis=-1, keepdims=True))
        pv = _dot_nt(p.astype(BF16), vt) / jnp.sum(p, axis=-1, keepdims=True)
        acc = acc + jnp.where(sel, pv, 0.0)
    o_ref[0] = acc


def _mem_attn(q, mkt, mvt, layer, tq):
    b, t, _ = q.shape
    m = mkt.shape[2]
    base = layer * b
    return pl.pallas_call(
        _mem_attn_kernel,
        out_shape=jax.ShapeDtypeStruct((b, t, MEM_W), F32),
        grid=(b, t // tq),
        in_specs=[pl.BlockSpec((1, tq, MEM_W), lambda i, j: (i, j, 0)),
                  pl.BlockSpec((1, MEM_W, m), lambda i, j: (base + i, 0, 0)),
                  pl.BlockSpec((1, MEM_W, m), lambda i, j: (base + i, 0, 0))],
        out_specs=pl.BlockSpec((1, tq, MEM_W), lambda i, j: (i, j, 0)),
        compiler_params=_params(("parallel", "parallel")),
        name="mem_attention",
    )(q, mkt, mvt)


def _a_out_kernel(x_ref, o_ref, r_ref, go_ref, om_ref, wa_ref, wm_ref, y_ref):
    o = o_ref[...]
    parts = []
    for h in range(GLA_HEADS):
        oh = o[:, h * DV_PAD:(h + 1) * DV_PAD]
        ms = jnp.sum(oh * oh, axis=-1, keepdims=True) * (1.0 / GLA_DV)
        parts.append(oh * lax.rsqrt(ms + EPS))
    on = jnp.concatenate(parts, axis=-1) * go_ref[...]
    main = (on * r_ref[...]).astype(BF16)
    y_ref[...] = (x_ref[...] + _dot(main, wa_ref[...])
                  + _dot(om_ref[...].astype(BF16), wm_ref[...]))


def _a_out(x, o, r, go, om, wa, wm, tm):
    n = x.shape[0]
    row = lambda c: pl.BlockSpec((tm, c), lambda i: (i, 0))
    full = lambda a: pl.BlockSpec(a.shape, lambda i: (0,) * a.ndim)
    return pl.pallas_call(
        _a_out_kernel,
        out_shape=jax.ShapeDtypeStruct((n, D_MODEL), F32),
        grid=(n // tm,),
        in_specs=[row(D_MODEL), row(o.shape[1]), row(r.shape[1]), full(go), row(MEM_W),
                  full(wa), full(wm)],
        out_specs=row(D_MODEL),
        compiler_params=_params(("parallel",)),
        name="layer_a_out",
    )(x, o, r, go, om, wa, wm)


def _b_out_kernel(x_ref, o_ref, om_ref, wa_ref, wm_ref, y_ref):
    y_ref[...] = (x_ref[...] + _dot(o_ref[...].astype(BF16), wa_ref[...])
                  + _dot(om_ref[...].astype(BF16), wm_ref[...]))


def _b_out(x, o, om, wa, wm, tm):
    n = x.shape[0]
    row = lambda c: pl.BlockSpec((tm, c), lambda i: (i, 0))
    full = lambda a: pl.BlockSpec(a.shape, lambda i: (0,) * a.ndim)
    return pl.pallas_call(
        _b_out_kernel,
        out_shape=jax.ShapeDtypeStruct((n, D_MODEL), F32),
        grid=(n // tm,),
        in_specs=[row(D_MODEL), row(o.shape[1]), row(MEM_W), full(wa), full(wm)],
        out_specs=row(D_MODEL),
        compiler_params=_params(("parallel",)),
        name="layer_b_out",
    )(x, o, om, wa, wm)


L_C0 = 0
L_A0 = MLA_KV_RANK
L_B0 = L_A0 + LANE
L_COLS = L_B0 + LANE
KV_W = MLA_HEADS * HEAD_PAD


def _latent_kernel(x_ref, g_ref, wl_ref, gc_ref, ga_ref, gb_ref, cos_ref, sin_ref,
                   wuk_ref, gkn_ref, wuvt_ref, ones_ref,
                   c_ref, kpe_ref, kf_ref, vt_ref):
    hn = _rms(x_ref[...], g_ref[...]).astype(BF16)
    y = _dot(hn, wl_ref[...])
    c = _rms(y[:, L_C0:L_A0], gc_ref[...])
    c_ref[...] = c
    a = y[:, L_A0:L_B0]
    b = y[:, L_B0:L_COLS]
    r = lax.rsqrt(jnp.sum(a * a, axis=-1, keepdims=True) * (1.0 / MLA_DR) + EPS)
    kpe = a * r * ga_ref[...] * cos_ref[...] + b * r * gb_ref[...] * sin_ref[...]
    kpe_ref[...] = kpe
    cb = c.astype(BF16)
    kn = _dot(cb, wuk_ref[...])
    ones = ones_ref[...]
    gkn = gkn_ref[...]
    for h in range(MLA_HEADS):
        hs = slice(h * HEAD_PAD, (h + 1) * HEAD_PAD)
        knh = kn[:, hs]
        ms = _group_sum(knh * knh, ones) * (1.0 / MLA_DN)
        kf_ref[:, hs] = (knh * lax.rsqrt(ms + EPS) * gkn + kpe).astype(BF16)
    vt = _dot_nt(wuvt_ref[...], cb)
    slab_row = lax.broadcasted_iota(jnp.int32, vt.shape, 0) % HEAD_PAD
    vt_ref[...] = jnp.where(slab_row == V_ONE, 1.0, vt).astype(BF16)


def _latent(x, g, wl, gc, ga, gb, cos, sin, wuk, gkn, wuvt, ones_h, tm):
    n = x.shape[0]
    row = lambda c: pl.BlockSpec((tm, c), lambda i: (i, 0))
    full = lambda a: pl.BlockSpec(a.shape, lambda i: (0,) * a.ndim)
    return pl.pallas_call(
        _latent_kernel,
        out_shape=(jax.ShapeDtypeStruct((n, MLA_KV_RANK), F32), jax.ShapeDtypeStruct((n, LANE), F32),
                   jax.ShapeDtypeStruct((n, KV_W), BF16), jax.ShapeDtypeStruct((KV_W, n), BF16)),
        grid=(n // tm,),
        in_specs=[row(D_MODEL), full(g), full(wl), full(gc), full(ga), full(gb), row(LANE), row(LANE),
                  full(wuk), full(gkn), full(wuvt), full(ones_h)],
        out_specs=(row(MLA_KV_RANK), row(LANE), row(KV_W), pl.BlockSpec((KV_W, tm), lambda i: (0, i))),
        compiler_params=_params(("parallel",)),
        name="shared_latent",
    )(x, g, wl, gc, ga, gb, cos, sin, wuk, gkn, wuvt, ones_h)


B_COLS = MLA_Q_RANK + MEM_W


def _b_in_kernel(x_ref, g_ref, w_ref, gqa_ref, w1_ref, w2_ref, g1_ref, g2_ref, inv_ref,
                 cos_ref, sin_ref, gq_ref, onesh_ref, ones64_ref, q_ref, mq_ref):
    u = _rms(x_ref[...], g_ref[...]).astype(BF16)
    p = _dot(u, w_ref[...])
    cq = _rms(p[:, :MLA_Q_RANK], gqa_ref[...]).astype(BF16)
    q1 = _dot(cq, w1_ref[...])
    q2 = _dot(cq, w2_ref[...])
    onesh = onesh_ref[...]
    inv = inv_ref[...]
    c1 = g1_ref[...] * cos_ref[...]
    c2 = g2_ref[...] * sin_ref[...]
    for h in range(MLA_HEADS):
        hs = slice(h * HEAD_PAD, (h + 1) * HEAD_PAD)
        q1h = q1[:, hs]
        rs = lax.rsqrt(_group_sum(q1h * q1h, onesh) * inv + EPS)
        q_ref[:, hs] = ((q1h * c1 + q2[:, hs] * c2) * rs).astype(BF16)
    mq_ref[...] = _head_rms(p[:, MLA_Q_RANK:B_COLS], ones64_ref[...], MEM_DH, gq_ref[...])


def _b_in(x, g, w, gqa, w1, w2, g1, g2, inv, cos, sin, gq, ones_h, ones64, tm):
    n = x.shape[0]
    row = lambda c: pl.BlockSpec((tm, c), lambda i: (i, 0))
    full = lambda a: pl.BlockSpec(a.shape, lambda i: (0,) * a.ndim)
    return pl.pallas_call(
        _b_in_kernel,
        out_shape=(jax.ShapeDtypeStruct((n, KV_W), BF16), jax.ShapeDtypeStruct((n, MEM_W), F32)),
        grid=(n // tm,),
        in_specs=[row(D_MODEL), full(g), full(w), full(gqa), full(w1), full(w2), full(g1), full(g2),
                  full(inv), row(LANE), row(LANE), full(gq), full(ones_h), full(ones64)],
        out_specs=(row(KV_W), row(MEM_W)),
        compiler_params=_params(("parallel",)),
        name="layer_b_in",
    )(x, g, w, gqa, w1, w2, g1, g2, inv, cos, sin, gq, ones_h, ones64)


def _flash_kernel(q_ref, k_ref, vt_ref, o_ref, sa_sc, sb_sc, m_sc, acc_sc):
    qi = pl.program_id(2)
    tq = q_ref.shape[1]
    tk = sa_sc.shape[0]
    q = q_ref[0]

    def scores_t(kb):
        start = pl.multiple_of(kb * tk, tk)
        return _dot_nt(k_ref[0, pl.ds(start, tk), :], q)

    def accumulate(s, kb, diagonal):
        if diagonal is not None:
            key = diagonal + lax.broadcasted_iota(jnp.int32, (tk, tq), 0)
            qry = lax.broadcasted_iota(jnp.int32, (tk, tq), 1)
            s = jnp.where(key <= qry, s, NEG)
        m = m_sc[...]
        m_new = jnp.maximum(m, jnp.max(s, axis=0, keepdims=True))
        p = jnp.exp2(s - m_new).astype(BF16)
        start = pl.multiple_of(kb * tk, tk)
        acc_sc[...] = jnp.exp2(m - m_new) * acc_sc[...] + _dot(vt_ref[:, pl.ds(start, tk)], p)
        m_sc[...] = m_new

    m_sc[...] = jnp.full_like(m_sc, NEG)
    acc_sc[...] = jnp.zeros_like(acc_sc)
    sa_sc[...] = scores_t(0)

    def pair(i, carry):
        kb = 2 * i
        sb_sc[...] = scores_t(kb + 1)
        accumulate(sa_sc[...], kb, None)
        sa_sc[...] = scores_t(kb + 2)
        accumulate(sb_sc[...], kb + 1, None)
        return carry

    lax.fori_loop(0, qi, pair, 0)
    sb_sc[...] = scores_t(2 * qi + 1)
    accumulate(sa_sc[...], 2 * qi, 0)
    accumulate(sb_sc[...], 2 * qi + 1, tk)
    acc = acc_sc[...]
    o_ref[0] = (acc / acc[V_ONE:V_ONE + 1, :]).T.astype(o_ref.dtype)


def _flash(q, k, vt, tq):
    b, t, _ = q.shape
    tk = tq // 2
    qspec = pl.BlockSpec((1, tq, HEAD_PAD), lambda bi, h, i: (bi, i, h))
    return pl.pallas_call(
        _flash_kernel,
        out_shape=jax.ShapeDtypeStruct((b, t, KV_W), BF16),
        grid=(b, MLA_HEADS, t // tq),
        in_specs=[qspec,
                  pl.BlockSpec((1, t, HEAD_PAD), lambda bi, h, i: (bi, 0, h)),
                  pl.BlockSpec((HEAD_PAD, t), lambda bi, h, i: (h, bi))],
        out_specs=qspec,
        scratch_shapes=[pltpu.VMEM((tk, tq), F32), pltpu.VMEM((tk, tq), F32),
                        pltpu.VMEM((1, tq), F32), pltpu.VMEM((HEAD_PAD, tq), F32)],
        compiler_params=_params(("parallel", "parallel", "arbitrary")),
        name="prompt_latent_attention",
    )(q, k, vt)


PAGES_PER_STEP = 32
QROWS = MLA_HEADS * T_PAD
SUB = 512


def _absorb_kernel(q_ref, w_ref, o_ref):
    o_ref[0] = _dot(q_ref[...], w_ref[0]).astype(BF16)


def _absorb(q, wabs):
    n = q.shape[0]
    return pl.pallas_call(
        _absorb_kernel,
        out_shape=jax.ShapeDtypeStruct((MLA_HEADS, n, MLA_KV_RANK), BF16),
        grid=(MLA_HEADS,),
        in_specs=[pl.BlockSpec((n, HEAD_PAD), lambda h: (0, h)),
                  pl.BlockSpec((1, HEAD_PAD, MLA_KV_RANK), lambda h: (h, 0, 0))],
        out_specs=pl.BlockSpec((1, n, MLA_KV_RANK), lambda h: (h, 0, 0)),
        compiler_params=_params(("parallel",)),
        name="absorb_w_uk",
    )(q, wabs)


def _scores(kt, sn, sp):
    rows = []
    for h in range(MLA_HEADS):
        kth = kt[h * MLA_DN:(h + 1) * MLA_DN, :]
        r = lax.rsqrt(jnp.sum(kth * kth, axis=0, keepdims=True) * (1.0 / MLA_DN) + EPS)
        rows.append(sn[h * T_PAD:(h + 1) * T_PAD, :] * r)
    return jnp.concatenate(rows, axis=0) + sp


def _online_softmax(s, cb, m, l, acc):
    m_new = jnp.maximum(m, jnp.max(s, axis=-1, keepdims=True))
    alpha = jnp.exp2(m - m_new)
    p = jnp.exp2(s - m_new)
    l = alpha * l + jnp.sum(p, axis=-1, keepdims=True)
    acc = alpha * acc + _dot(p.astype(BF16), cb)
    return m_new, l, acc


def _sample_attn_kernel(pt_ref, ckv_hbm, kpe_hbm, qa_ref, qp_ref, wukt_ref, cn_ref, kn_ref, o_ref,
                        cbuf, kbuf, csem, ksem, kt_sc, sn_sc, sp_sc, cb_sc, *, n_chunks, ch_pages, sub):
    b = pl.program_id(0)
    page = cbuf.shape[1] // ch_pages
    n_sub = ch_pages * page // sub
    qa = qa_ref[0]
    qp = qp_ref[0]
    wukt = wukt_ref[...]

    def chunk_copies(seq, ch, slot):
        copies = []
        for i in range(ch_pages):
            pg = pt_ref[seq, ch * ch_pages + i]
            rows = pl.ds(i * page, page)
            copies.append(pltpu.make_async_copy(ckv_hbm.at[pg], cbuf.at[slot, rows, :], csem.at[slot]))
            copies.append(pltpu.make_async_copy(kpe_hbm.at[pg], kbuf.at[slot, :, rows], ksem.at[slot]))
        return copies

    def stage1(slot, i, buf):
        cb = cbuf[slot, i * sub:(i + 1) * sub, :].astype(BF16)
        cb_sc[buf] = cb
        kt_sc[buf] = _dot_nt(wukt, cb)
        sn_sc[buf] = _dot_nt(qa, cb)
        sp_sc[buf] = _dot(qp, kbuf[slot, :, i * sub:(i + 1) * sub].astype(BF16))

    def stage2(buf, state):
        return _online_softmax(_scores(kt_sc[buf], sn_sc[buf], sp_sc[buf]), cb_sc[buf], *state)

    for c in chunk_copies(b, 0, 0):
        c.start()

    cn = cn_ref[0].astype(BF16)
    s = _scores(_dot_nt(wukt, cn), _dot_nt(qa, cn), _dot(qp, kn_ref[0].astype(BF16)))
    t_q = lax.broadcasted_iota(jnp.int32, s.shape, 0) % T_PAD
    t_k = lax.broadcasted_iota(jnp.int32, s.shape, 1)
    init = (jnp.full((QROWS, 1), NEG, F32), jnp.zeros((QROWS, 1), F32), jnp.zeros((QROWS, MLA_KV_RANK), F32))
    state = _online_softmax(jnp.where(t_k <= t_q, s, NEG), cn, *init)

    def chunk(ch, slot, state):
        for c in chunk_copies(b, ch, slot):
            c.wait()
        for c in chunk_copies(b, (ch + 1) % n_chunks, 1 - slot):
            c.start()
        stage1(slot, 0, 0)
        for i in range(n_sub):
            if i + 1 < n_sub:
                stage1(slot, i + 1, (i + 1) % 2)
            state = stage2(i % 2, state)
        return state

    def chunk_pair(pi, state):
        return chunk(2 * pi + 1, 1, chunk(2 * pi, 0, state))

    _, l, acc = lax.fori_loop(0, n_chunks // 2, chunk_pair, state)
    for c in chunk_copies(b, 0, 0):
        c.wait()
    o_ref[0] = acc / l


def _sample_attn(page_table, cache_ckv, cache_kpe_t, qa, qp, wukt, c_new, k_new_t):
    nb, n_pages = page_table.shape
    page = cache_ckv.shape[1]
    ch_pages = math.gcd(n_pages // 2, PAGES_PER_STEP)
    n_chunks = n_pages // ch_pages
    tokens = ch_pages * page
    sub = min(SUB, tokens // 2)
    assert tokens % (2 * sub) == 0 and n_chunks % 2 == 0
    per_b = lambda r, w: pl.BlockSpec((1, r, w), lambda b, pt: (b, 0, 0))
    hbm = pl.BlockSpec(memory_space=pl.ANY)
    return pl.pallas_call(
        functools.partial(_sample_attn_kernel, n_chunks=n_chunks, ch_pages=ch_pages, sub=sub),
        out_shape=jax.ShapeDtypeStruct((nb, QROWS, MLA_KV_RANK), F32),
        grid_spec=pltpu.PrefetchScalarGridSpec(
            num_scalar_prefetch=1,
            grid=(nb,),
            in_specs=[hbm, hbm, per_b(QROWS, MLA_KV_RANK), per_b(QROWS, MLA_DR),
                      pl.BlockSpec(wukt.shape, lambda b, pt: (0, 0)),
                      per_b(c_new.shape[1], MLA_KV_RANK), per_b(MLA_DR, k_new_t.shape[2])],
            out_specs=per_b(QROWS, MLA_KV_RANK),
            scratch_shapes=[pltpu.VMEM((2, tokens, MLA_KV_RANK), F32),
                            pltpu.VMEM((2, MLA_DR, tokens), F32),
                            pltpu.SemaphoreType.DMA((2,)), pltpu.SemaphoreType.DMA((2,)),
                            pltpu.VMEM((2, MLA_HEADS * MLA_DN, sub), F32),
                            pltpu.VMEM((2, QROWS, sub), F32),
                            pltpu.VMEM((2, QROWS, sub), F32),
                            pltpu.VMEM((2, sub, MLA_KV_RANK), BF16)],
        ),
        compiler_params=_params(("arbitrary",)),
        name="sample_latent_attention",
    )(page_table, cache_ckv, cache_kpe_t, qa, qp, wukt, c_new, k_new_t)


def _uv_kernel(x_ref, w_ref, o_ref):
    o_ref[0] = _dot(x_ref[0].astype(BF16), w_ref[0])


def _uv_project(ctx, wuv):
    _, n, _ = ctx.shape
    return pl.pallas_call(
        _uv_kernel,
        out_shape=jax.ShapeDtypeStruct((MLA_HEADS, n, HEAD_PAD), F32),
        grid=(MLA_HEADS,),
        in_specs=[pl.BlockSpec((1, n, MLA_KV_RANK), lambda h: (h, 0, 0)),
                  pl.BlockSpec((1, MLA_KV_RANK, HEAD_PAD), lambda h: (h, 0, 0))],
        out_specs=pl.BlockSpec((1, n, HEAD_PAD), lambda h: (h, 0, 0)),
        compiler_params=_params(("parallel",)),
        name="apply_w_uv",
    )(ctx, wuv)


def _memkv_kernel(m_ref, g_ref, wkt_ref, wvt_ref, gk_ref, ones_ref, kt_ref, vt_ref):
    mn = _rms(m_ref[0], g_ref[0]).astype(BF16)
    kt = _dot_nt(wkt_ref[0], mn)
    hi, mid, _ = _split3(kt * kt)
    ones = ones_ref[...]
    ms = (_dot(ones, hi) + _dot(ones, mid)) * (1.0 / MEM_DH)
    kt_ref[0, 0] = kt * lax.rsqrt(ms + EPS) * gk_ref[0]
    vt_ref[0, 0] = _dot_nt(wvt_ref[0], mn)


def _memkv(mem, g, wkt, wvt, gk, ones64):
    nb, m, _ = mem.shape
    nl = wkt.shape[0]
    lay = lambda a: pl.BlockSpec((1,) + a.shape[1:], lambda l, b: (l,) + (0,) * (a.ndim - 1))
    out = pl.BlockSpec((1, 1, MEM_W, m), lambda l, b: (l, b, 0, 0))
    shape = jax.ShapeDtypeStruct((nl, nb, MEM_W, m), F32)
    return pl.pallas_call(
        _memkv_kernel,
        out_shape=(shape, shape),
        grid=(nl, nb),
        in_specs=[pl.BlockSpec((1, m, D_MODEL), lambda l, b: (b, 0, 0)), lay(g), lay(wkt), lay(wvt), lay(gk),
                  pl.BlockSpec(ones64.shape, lambda l, b: (0, 0))],
        out_specs=(out, out),
        compiler_params=_params(("parallel", "parallel")),
        name="prompt_mem_kv",
    )(mem, g, wkt, wvt, gk, ones64)


def _pad_heads(w, heads, size, padded, axis=-1):
    axis = axis % w.ndim
    shape = w.shape[:axis] + (heads, size) + w.shape[axis + 1:]
    w = w.reshape(shape)
    pad = [(0, 0)] * w.ndim
    pad[axis + 1] = (0, padded - size)
    w = jnp.pad(w, pad)
    return w.reshape(w.shape[:axis] + (heads * padded,) + w.shape[axis + 2:])


def _block_ones(n, sizes):
    gid = []
    g = 0
    while len(gid) < n:
        for s in sizes:
            gid += [g] * s
            g += 1
    gid = jnp.asarray(gid[:n], jnp.int32)
    return (gid[:, None] == gid[None, :]).astype(BF16)


def _rope_tables(pos):
    half = MLA_DR // 2
    inv = ROPE_THETA ** (-jnp.arange(half, dtype=F32) / half)
    ang = pos.astype(F32)[:, None] * inv
    cos, sin = jnp.cos(ang), jnp.sin(ang)
    z = jnp.zeros((pos.shape[0], MLA_DN), F32)
    z2 = jnp.zeros((pos.shape[0], HEAD_PAD - MLA_DN - MLA_DR), F32)
    return (jnp.concatenate([z + 1.0, cos, cos, z2], axis=-1), jnp.concatenate([z, -sin, sin, z2], axis=-1))


def _swap_halves(w):
    half = w.shape[-1] // 2
    return jnp.concatenate([w[..., half:], w[..., :half]], axis=-1)


def _rope_slab(w):
    pad = [(0, 0)] * (w.ndim - 1) + [(MLA_DN, HEAD_PAD - MLA_DN - MLA_DR)]
    return jnp.pad(w, pad)


def _tile_for(n, pref):
    t = min(n, pref)
    while n % t:
        t //= 2
    return t


def kernel(x_prompt, x_sample, mem_prompt, cache_mem_k, cache_mem_v, state_gla, cache_ckv, cache_kpe, page_table,
           ffn1_norm, ffn1_w_gate, ffn1_w_up, ffn1_w_down, ffn2_norm, ffn2_w_gate, ffn2_w_up, ffn2_w_down,
           mix_norm, w_out, mem_norm, w_mem_k, w_mem_v, mem_k_norm, mem_q_norm,
           gla_w_in, gla_w_gate2, gla_b_gate, gla_o_norm,
           mla_w_in, mla_q_a_norm, mla_w_uq, mla_q_nope_norm, mla_q_pe_norm,
           kv_norm, kv_w_dkv, kv_ckv_norm, kv_w_kr, kv_kpe_norm, kv_w_uk, kv_k_nope_norm, kv_w_uv):
    bp, tp, _ = x_prompt.shape
    bs, ts, _ = x_sample.shape
    n_mem = mem_prompt.shape[1]
    past = page_table.shape[1] * cache_ckv.shape[1]
    row = lambda a: a.reshape(1, -1).astype(F32)
    bf = lambda a: a.astype(BF16)

    ffn = [[(row(n[l]), bf(g[l]), bf(u[l]), bf(d[l])) for l in range(2)]
           for n, g, u, d in ((ffn1_norm, ffn1_w_gate, ffn1_w_up, ffn1_w_down),
                              (ffn2_norm, ffn2_w_gate, ffn2_w_up, ffn2_w_down))]
    ones64 = _block_ones(MEM_W, [MEM_DH])
    ones_head = _block_ones(HEAD_PAD, [MLA_DN, MLA_DR, HEAD_PAD - MLA_DN - MLA_DR])
    gmq = [row(jnp.tile(mem_q_norm[l], MEM_HEADS)) for l in range(2)]

    wi = gla_w_in[0]
    o_q, o_k, o_v, o_g, o_r, o_m = 0, GLA_QK, 2 * GLA_QK, 2 * GLA_QK + GLA_V, 2 * GLA_QK + GLA_V + GLA_GATE_RANK, \
        2 * GLA_QK + 2 * GLA_V + GLA_GATE_RANK
    wa_in = bf(jnp.concatenate([
        _pad_heads(wi[:, o_q:o_k], GLA_HEADS, GLA_DK, DK_PAD),
        _pad_heads(wi[:, o_k:o_v], GLA_HEADS, GLA_DK, DK_PAD),
        _pad_heads(wi[:, o_v:o_g], GLA_HEADS, GLA_DV, DV_PAD),
        _pad_heads(wi[:, o_r:o_m], GLA_HEADS, GLA_DV, DV_PAD),
        wi[:, o_m:],
        jnp.pad(wi[:, o_g:o_r], ((0, 0), (0, LANE - GLA_GATE_RANK))),
    ], axis=1))
    w_gate2 = bf(jnp.pad(_pad_heads(gla_w_gate2[0], GLA_HEADS, GLA_DK, DK_PAD),
                         ((0, LANE - GLA_GATE_RANK), (0, 0))))
    b_gate = row(_pad_heads(gla_b_gate[0], GLA_HEADS, GLA_DK, DK_PAD))
    g_o = row(jnp.pad(gla_o_norm[0], (0, DV_PAD - GLA_DV)))
    g_o = jnp.tile(g_o, (1, GLA_HEADS))
    wa_out_main = bf(_pad_heads(w_out[0][:GLA_V], GLA_HEADS, GLA_DV, DV_PAD, axis=0))
    wa_out_mem = bf(w_out[0][GLA_V:])

    kr_slab = _rope_slab(kv_w_kr)
    w_lat = bf(jnp.concatenate([kv_w_dkv, kr_slab, _rope_slab(_swap_halves(kv_w_kr))], axis=1))
    g_kpe_a = row(_rope_slab(kv_kpe_norm))
    g_kpe_b = row(_rope_slab(_swap_halves(kv_kpe_norm)))
    w_uk_pad = bf(_pad_heads(kv_w_uk, MLA_HEADS, MLA_DN, HEAD_PAD))
    w_uv_t = bf(_pad_heads(kv_w_uv, MLA_HEADS, MLA_DV, HEAD_PAD).T)
    g_kn_slab = row(jnp.pad(kv_k_nope_norm, (0, HEAD_PAD - MLA_DN)))

    scale = (MLA_DN + MLA_DR) ** -0.5 * LOG2E
    wuq = mla_w_uq[0].reshape(MLA_Q_RANK, MLA_HEADS, MLA_DN + MLA_DR)
    w_q1 = bf(jnp.pad(wuq, ((0, 0), (0, 0), (0, HEAD_PAD - MLA_DN - MLA_DR))).reshape(MLA_Q_RANK, KV_W))
    w_q2 = bf(_rope_slab(_swap_halves(wuq[..., MLA_DN:])).reshape(MLA_Q_RANK, KV_W))
    g_q1 = row(jnp.concatenate([mla_q_nope_norm[0], mla_q_pe_norm[0],
                                jnp.zeros((HEAD_PAD - MLA_DN - MLA_DR,), F32)])) * scale
    g_q2 = row(_rope_slab(_swap_halves(mla_q_pe_norm[0]))) * scale
    inv_sizes = row(jnp.concatenate([jnp.full((MLA_DN,), 1.0 / MLA_DN, F32),
                                     jnp.full((HEAD_PAD - MLA_DN,), 1.0 / MLA_DR, F32)]))
    wb_in = bf(mla_w_in[0])
    wb_out_main = bf(_pad_heads(w_out[1][:MLA_HEADS * MLA_DV], MLA_HEADS, MLA_DV, HEAD_PAD, axis=0))
    wb_out_mem = bf(w_out[1][MLA_HEADS * MLA_DV:])
    wuk_h = kv_w_uk.reshape(MLA_KV_RANK, MLA_HEADS, MLA_DN).transpose(1, 2, 0)
    w_abs = bf(jnp.pad(wuk_h * kv_k_nope_norm[None, :, None], ((0, 0), (0, HEAD_PAD - MLA_DN), (0, 0))))
    w_uk_t = bf(kv_w_uk.T)
    w_uv_h = bf(jnp.pad(kv_w_uv.reshape(MLA_KV_RANK, MLA_HEADS, MLA_DV).transpose(1, 0, 2),
                        ((0, 0), (0, 0), (0, HEAD_PAD - MLA_DV))))

    g_mem_k = jnp.broadcast_to(jnp.tile(mem_k_norm, (1, MEM_HEADS))[:, :, None], (2, MEM_W, n_mem))
    mem_kt_p, mem_vt_p = _memkv(mem_prompt, mem_norm.reshape(2, 1, D_MODEL),
                                bf(w_mem_k.swapaxes(1, 2)), bf(w_mem_v.swapaxes(1, 2)), g_mem_k, ones64)

    def mem_out(a):
        return a.reshape(2, bp, MEM_HEADS, MEM_DH, n_mem).transpose(0, 1, 4, 2, 3)

    def trunk(x, nb, t, pos, mem_k, mem_v, s0, attend):
        n = nb * t
        tm = _tile_for(n, FFN_ROWS)
        tp_ = _tile_for(n, PROJ_ROWS)
        ta = _tile_for(n, A_IN_ROWS)
        seq_pad = (-t) % T_PAD
        tpad = t + seq_pad

        def to_seq(a):
            a = a.reshape(nb, t, a.shape[-1])
            return jnp.pad(a, ((0, 0), (0, seq_pad), (0, 0))) if seq_pad else a

        def from_seq(a):
            return a[:, :t].reshape(n, a.shape[-1])

        cos, sin = _rope_tables(pos)
        cos = jnp.tile(cos, (nb, 1))
        sin = jnp.tile(sin, (nb, 1))

        def mem_part(mq, l):
            return from_seq(_mem_attn(to_seq(mq), mem_k, mem_v, l, _tile_for(tpad, 1024)))

        x = _ffn_half(x, *ffn[0][0], tm)
        q, k, v, la, r, mq = _a_in(x, row(mix_norm[0]), wa_in, w_gate2, b_gate, gmq[0], ones64, ta)
        s0t = jnp.pad(s0.transpose(0, 1, 3, 2), ((0, 0), (0, 0), (0, DV_PAD - GLA_DV), (0, DK_PAD - GLA_DK)))
        o, st = _gla(to_seq(q), to_seq(k), to_seq(v), to_seq(la), s0t, math.gcd(tpad, GLA_CHUNK))
        gla_state = st[:, :, :GLA_DV, :GLA_DK].transpose(0, 1, 3, 2)
        x = _a_out(x, from_seq(o), r, g_o, mem_part(mq, 0), wa_out_main, wa_out_mem, tp_)
        x = _ffn_half(x, *ffn[1][0], tm)
        c, kpe, k_full, v_full = _latent(x, row(kv_norm), w_lat, row(kv_ckv_norm), g_kpe_a, g_kpe_b, cos, sin,
                                         w_uk_pad, g_kn_slab, w_uv_t, ones_head, tp_)
        kpe = kpe[:, MLA_DN:MLA_DN + MLA_DR]
        x = _ffn_half(x, *ffn[0][1], tm)
        q_full, mq = _b_in(x, row(mix_norm[1]), wb_in, row(mla_q_a_norm[0]), w_q1, w_q2, g_q1, g_q2, inv_sizes,
                           cos, sin, gmq[1], ones_head, ones64, tp_)
        o_main = attend(q_full, k_full, v_full, c, kpe)
        x = _b_out(x, o_main, mem_part(mq, 1), wb_out_main, wb_out_mem, tp_)
        x = _ffn_half(x, *ffn[1][1], tm)
        return x, gla_state, c, kpe

    def attend_prompt(q_full, k_full, v_full, c, kpe):
        sh = (bp, tp, KV_W)
        o = _flash(q_full.reshape(sh), k_full.reshape(sh), v_full, _tile_for(tp, FLASH_ROWS))
        return o.reshape(bp * tp, KV_W)

    def attend_sample(q_full, k_full, v_full, c, kpe):
        n = bs * ts
        qa = _absorb(q_full, w_abs)

        def rows(a):
            a = a.reshape(MLA_HEADS, bs, ts, a.shape[-1]).transpose(1, 0, 2, 3)
            a = jnp.pad(a, ((0, 0), (0, 0), (0, T_PAD - ts), (0, 0)))
            return a.reshape(bs, QROWS, a.shape[-1])

        qp = q_full.reshape(n, MLA_HEADS, HEAD_PAD)[:, :, MLA_DN:MLA_DN + MLA_DR].transpose(1, 0, 2)
        page = cache_ckv.shape[1]
        c_new = jnp.pad(c.reshape(bs, ts, MLA_KV_RANK), ((0, 0), (0, page - ts), (0, 0)))
        k_new_t = jnp.pad(kpe.reshape(bs, ts, MLA_DR), ((0, 0), (0, page - ts), (0, 0))).swapaxes(1, 2)
        ctx = _sample_attn(page_table, cache_ckv, cache_kpe.swapaxes(1, 2), rows(qa), rows(qp), w_uk_t,
                           c_new, k_new_t)
        ctx = ctx.reshape(bs, MLA_HEADS, T_PAD, MLA_KV_RANK).transpose(1, 0, 2, 3)
        o = _uv_project(ctx.reshape(MLA_HEADS, bs * T_PAD, MLA_KV_RANK), w_uv_h)
        o = o.reshape(MLA_HEADS, bs, T_PAD, HEAD_PAD)[:, :, :ts].transpose(1, 2, 0, 3)
        return o.reshape(n, KV_W)

    mk_p = mem_kt_p.reshape(2 * bp, MEM_W, n_mem)
    mv_p = mem_vt_p.reshape(2 * bp, MEM_W, n_mem)
    s0_p = jnp.zeros((bp, GLA_HEADS, GLA_DK, GLA_DV), F32)
    y_p, st_p, c_p, kpe_p = trunk(x_prompt.reshape(bp * tp, D_MODEL), bp, tp, jnp.arange(tp), mk_p, mv_p,
                                  s0_p, attend_prompt)

    def mem_in(a):
        return a.transpose(0, 1, 3, 4, 2).reshape(2 * bs, MEM_W, n_mem)

    mk_s = mem_in(cache_mem_k)
    mv_s = mem_in(cache_mem_v)
    y_s, st_s, c_s, kpe_s = trunk(x_sample.reshape(bs * ts, D_MODEL), bs, ts, past + jnp.arange(ts), mk_s, mv_s,
                                  state_gla[0], attend_sample)

    return (y_p.reshape(bp, tp, D_MODEL), y_s.reshape(bs, ts, D_MODEL),
            st_p[None], st_s[None],
            c_p.reshape(bp, tp, MLA_KV_RANK), kpe_p.reshape(bp, tp, MLA_DR),
            c_s.reshape(bs, ts, MLA_KV_RANK), kpe_s.reshape(bs, ts, MLA_DR),
            mem_out(mem_kt_p), mem_out(mem_vt_p))
```

```python
import functools
import math

import jax
import jax.numpy as jnp
from jax import lax
from jax.experimental import pallas as pl
from jax.experimental.pallas import tpu as pltpu

F32 = jnp.float32
BF16 = jnp.bfloat16

D_MODEL = 1024
D_FF = 2816
GLA_HEADS = 4
GLA_DK = 96
GLA_DV = 192
GLA_QK = GLA_HEADS * GLA_DK
GLA_V = GLA_HEADS * GLA_DV
GLA_GATE_RANK = 16
GLA_GATE_TEMP = 16.0
MEM_HEADS = 4
MEM_DH = 64
MEM_W = MEM_HEADS * MEM_DH
MLA_HEADS = 12
MLA_DN = 64
MLA_DR = 32
MLA_DV = 64
MLA_KV_RANK = 256
MLA_Q_RANK = 384
ROPE_THETA = 10000.0
EPS = 1e-6

LANE = 128
DK_PAD = LANE
DV_PAD = 2 * LANE
HEAD_PAD = LANE
V_ONE = MLA_DV
LOG2E = math.log2(math.e)
GLA_CHUNK = 64
GLA_SUB = 8
GLA_SEQS = 2
T_PAD = 8
NEG = -1e30
VMEM_LIMIT = 56 * 1024 * 1024
FFN_ROWS = 1024
FLASH_ROWS = 1024
FFN_COLS = 256
PROJ_ROWS = 512
A_IN_ROWS = 256

NT_DIMS = (((1,), (1,)), ((), ()))
TN_DIMS = (((0,), (0,)), ((), ()))


def _dot(a, b):
    return jnp.dot(a, b, preferred_element_type=F32)


def _dot_nt(a, b):
    return lax.dot_general(a, b, NT_DIMS, preferred_element_type=F32)


def _dot_tn(a, b):
    return lax.dot_general(a, b, TN_DIMS, preferred_element_type=F32)


def _rms(x, g):
    return x * lax.rsqrt(jnp.mean(x * x, axis=-1, keepdims=True) + EPS) * g


def _split3(x):
    hi = x.astype(BF16)
    r1 = x - hi.astype(F32)
    mid = r1.astype(BF16)
    lo = (r1 - mid.astype(F32)).astype(BF16)
    return hi, mid, lo


def _group_sum(x, ones_bd):
    hi, mid, _ = _split3(x)
    return _dot(hi, ones_bd) + _dot(mid, ones_bd)


def _params(sem):
    return pltpu.CompilerParams(dimension_semantics=sem, vmem_limit_bytes=VMEM_LIMIT)


def _ffn_kernel(x_ref, g_ref, wg_ref, wu_ref, wd_ref, o_ref, act_sc):
    x = x_ref[...]
    h = _rms(x, g_ref[...]).astype(BF16)
    for f in range(D_FF // FFN_COLS):
        cols = slice(f * FFN_COLS, (f + 1) * FFN_COLS)
        gate = _dot(h, wg_ref[:, cols])
        up = _dot(h, wu_ref[:, cols])
        act_sc[:, cols] = (gate * jax.nn.sigmoid(gate) * up).astype(BF16)
    o_ref[...] = x + 0.5 * _dot(act_sc[...], wd_ref[...])


def _ffn_half(x, g, wg, wu, wd, tm):
    n = x.shape[0]
    resident = lambda a: pl.BlockSpec(a.shape, lambda i: (0,) * a.ndim, pipeline_mode=pl.Buffered(1))
    return pl.pallas_call(
        _ffn_kernel,
        out_shape=jax.ShapeDtypeStruct((n, D_MODEL), F32),
        grid=(n // tm,),
        in_specs=[pl.BlockSpec((tm, D_MODEL), lambda i: (i, 0)), resident(g), resident(wg), resident(wu),
                  resident(wd)],
        out_specs=pl.BlockSpec((tm, D_MODEL), lambda i: (i, 0)),
        scratch_shapes=[pltpu.VMEM((tm, D_FF), BF16)],
        compiler_params=_params(("parallel",)),
        name="ffn_half",
    )(x, g, wg, wu, wd)


A_Q0 = 0
A_K0 = A_Q0 + GLA_HEADS * DK_PAD
A_V0 = A_K0 + GLA_HEADS * DK_PAD
A_R0 = A_V0 + GLA_HEADS * DV_PAD
A_M0 = A_R0 + GLA_HEADS * DV_PAD
A_G0 = A_M0 + MEM_W
A_COLS = A_G0 + LANE


def _head_rms(x, ones_bd, size, g):
    return x * lax.rsqrt(_group_sum(x * x, ones_bd) * (1.0 / size) + EPS) * g


def _a_in_kernel(x_ref, g_ref, w_ref, w2_ref, b2_ref, gq_ref, ones_ref,
                 q_ref, k_ref, v_ref, la_ref, r_ref, mq_ref):
    u = _rms(x_ref[...], g_ref[...]).astype(BF16)
    p = _dot(u, w_ref[...])
    q_ref[...] = p[:, A_Q0:A_K0] * (GLA_DK ** -0.5)
    k_ref[...] = p[:, A_K0:A_V0]
    v_ref[...] = p[:, A_V0:A_R0]
    r = p[:, A_R0:A_M0]
    r_ref[...] = r * jax.nn.sigmoid(r)
    mq_ref[...] = _head_rms(p[:, A_M0:A_G0], ones_ref[...], MEM_DH, gq_ref[...])
    z = _dot(p[:, A_G0:A_COLS].astype(BF16), w2_ref[...]) + b2_ref[...]
    log_sig = jnp.minimum(z, 0.0) - jnp.log1p(jnp.exp(-jnp.abs(z)))
    la_ref[...] = log_sig * (1.0 / GLA_GATE_TEMP)


def _a_in(x, g, w, w2, b2, gq, ones64, tm):
    n = x.shape[0]
    qk = GLA_HEADS * DK_PAD
    vv = GLA_HEADS * DV_PAD
    row = lambda c: pl.BlockSpec((tm, c), lambda i: (i, 0))
    full = lambda a: pl.BlockSpec(a.shape, lambda i: (0,) * a.ndim)
    return pl.pallas_call(
        _a_in_kernel,
        out_shape=(jax.ShapeDtypeStruct((n, qk), F32), jax.ShapeDtypeStruct((n, qk), F32),
                   jax.ShapeDtypeStruct((n, vv), F32), jax.ShapeDtypeStruct((n, qk), F32),
                   jax.ShapeDtypeStruct((n, vv), F32), jax.ShapeDtypeStruct((n, MEM_W), F32)),
        grid=(n // tm,),
        in_specs=[row(D_MODEL), full(g), full(w), full(w2), full(b2), full(gq), full(ones64)],
        out_specs=(row(qk), row(qk), row(vv), row(qk), row(vv), row(MEM_W)),
        compiler_params=_params(("parallel",)),
        name="layer_a_in",
    )(x, g, w, w2, b2, gq, ones64)


def _gla_kernel(q_ref, k_ref, v_ref, la_ref, s0_ref, o_ref, s_out_ref, st_sc):
    c = pl.program_id(1)
    nseq, chunk, _ = q_ref.shape
    sub = min(GLA_SUB, chunk)
    n_sc = chunk // sub

    @pl.when(c == 0)
    def _():
        st_sc[...] = s0_ref[...]

    t_idx = lax.broadcasted_iota(jnp.int32, (chunk, chunk), 0)
    s_idx = lax.broadcasted_iota(jnp.int32, (chunk, chunk), 1)
    tri = jnp.where(s_idx <= t_idx, 1.0, 0.0).astype(BF16)
    later = [(s_idx // sub == j) & (t_idx // sub > j) for j in range(n_sc - 1)]
    diag = [(s_idx == t_idx - d) & (t_idx % sub >= d) for d in range(sub)]
    for g in range(nseq):
        hi, mid, lo = _split3(la_ref[g])
        a_cum = _dot(tri, hi) + _dot(tri, mid) + _dot(tri, lo)
        a_last = a_cum[chunk - 1:chunk, :]
        a_end = jnp.concatenate(
            [jnp.broadcast_to(a_cum[(j + 1) * sub - 1:(j + 1) * sub, :], (sub, a_cum.shape[1]))
             for j in range(n_sc)], axis=0)
        e_q = jnp.exp(a_cum)
        e_kend = jnp.exp(a_end - a_cum)
        e_kd = jnp.exp(a_last - a_cum)
        e_last = jnp.exp(a_last)
        q = q_ref[g]
        k = k_ref[g]
        v = v_ref[g]
        for h in range(GLA_HEADS):
            ks = slice(h * DK_PAD, (h + 1) * DK_PAD)
            vs = slice(h * DV_PAD, (h + 1) * DV_PAD)
            a = a_cum[:, ks]
            qf = q[:, ks]
            kf = k[:, ks]
            k_end = (kf * e_kend[:, ks]).astype(BF16)
            scores = jnp.zeros((chunk, chunk), F32)
            for j in range(n_sc - 1):
                e_j = a[(j + 1) * sub - 1:(j + 1) * sub, :]
                q_j = (qf * jnp.exp(jnp.minimum(a - e_j, 0.0))).astype(BF16)
                scores = scores + jnp.where(later[j], _dot_nt(q_j, k_end), 0.0)
            for d in range(sub):
                k_d = kf if d == 0 else pltpu.roll(kf, d, 0)
                a_d = a if d == 0 else pltpu.roll(a, d, 0)
                pair = jnp.sum(qf * k_d * jnp.exp(jnp.minimum(a - a_d, 0.0)), axis=-1, keepdims=True)
                scores = scores + jnp.where(diag[d], pair, 0.0)
            qh = (qf * e_q[:, ks]).astype(BF16)
            kd = (kf * e_kd[:, ks]).astype(BF16)
            vh = v[:, vs].astype(BF16)
            st = st_sc[g, h]
            o_ref[g, :, vs] = _dot(scores.astype(BF16), vh) + _dot_nt(qh, st.astype(BF16))
            st_sc[g, h] = st * e_last[:, ks] + _dot_tn(vh, kd)

    @pl.when(c == pl.num_programs(1) - 1)
    def _():
        s_out_ref[...] = st_sc[...]


def _gla(q, k, v, la, s0t, chunk):
    b, t, _ = q.shape
    qk = GLA_HEADS * DK_PAD
    vv = GLA_HEADS * DV_PAD
    nseq = math.gcd(b, GLA_SEQS)
    seq = lambda c: pl.BlockSpec((nseq, chunk, c), lambda i, j: (i, j, 0))
    st = pl.BlockSpec((nseq, GLA_HEADS, DV_PAD, DK_PAD), lambda i, j: (i, 0, 0, 0))
    return pl.pallas_call(
        _gla_kernel,
        out_shape=(jax.ShapeDtypeStruct((b, t, vv), F32),
                   jax.ShapeDtypeStruct((b, GLA_HEADS, DV_PAD, DK_PAD), F32)),
        grid=(b // nseq, t // chunk),
        in_specs=[seq(qk), seq(qk), seq(vv), seq(qk), st],
        out_specs=(seq(vv), st),
        scratch_shapes=[pltpu.VMEM((nseq, GLA_HEADS, DV_PAD, DK_PAD), F32)],
        compiler_params=_params(("parallel", "arbitrary")),
        name="gla_recurrence",
    )(q, k, v, la, s0t)


def _mem_attn_kernel(q_ref, kt_ref, vt_ref, o_ref):
    q = q_ref[0]
    kt = kt_ref[0].astype(BF16)
    vt = vt_ref[0].astype(BF16)
    head = lax.broadcasted_iota(jnp.int32, (1, MEM_W), 1) // MEM_DH
    acc = jnp.zeros(q.shape, F32)
    for h in range(MEM_HEADS):
        sel = head == h
        qh = jnp.where(sel, q, 0.0).astype(BF16)
        s = _dot(qh, kt) * (MEM_DH ** -0.5)
        p = jnp.exp(s - jnp.max(s, axis=-1, keepdims=True))
        pv = _dot_nt(p.astype(BF16), vt) / jnp.sum(p, axis=-1, keepdims=True)
        acc = acc + jnp.where(sel, pv, 0.0)
    o_ref[0] = acc


def _mem_attn(q, mkt, mvt, layer, tq):
    b, t, _ = q.shape
    m = mkt.shape[2]
    base = layer * b
    return pl.pallas_call(
        _mem_attn_kernel,
        out_shape=jax.ShapeDtypeStruct((b, t, MEM_W), F32),
        grid=(b, t // tq),
        in_specs=[pl.BlockSpec((1, tq, MEM_W), lambda i, j: (i, j, 0)),
                  pl.BlockSpec((1, MEM_W, m), lambda i, j: (base + i, 0, 0)),
                  pl.BlockSpec((1, MEM_W, m), lambda i, j: (base + i, 0, 0))],
        out_specs=pl.BlockSpec((1, tq, MEM_W), lambda i, j: (i, j, 0)),
        compiler_params=_params(("parallel", "parallel")),
        name="mem_attention",
    )(q, mkt, mvt)


def _a_out_kernel(x_ref, o_ref, r_ref, go_ref, om_ref, wa_ref, wm_ref, y_ref):
    o = o_ref[...]
    parts = []
    for h in range(GLA_HEADS):
        oh = o[:, h * DV_PAD:(h + 1) * DV_PAD]
        ms = jnp.sum(oh * oh, axis=-1, keepdims=True) * (1.0 / GLA_DV)
        parts.append(oh * lax.rsqrt(ms + EPS))
    on = jnp.concatenate(parts, axis=-1) * go_ref[...]
    main = (on * r_ref[...]).astype(BF16)
    y_ref[...] = (x_ref[...] + _dot(main, wa_ref[...])
                  + _dot(om_ref[...].astype(BF16), wm_ref[...]))


def _a_out(x, o, r, go, om, wa, wm, tm):
    n = x.shape[0]
    row = lambda c: pl.BlockSpec((tm, c), lambda i: (i, 0))
    full = lambda a: pl.BlockSpec(a.shape, lambda i: (0,) * a.ndim)
    return pl.pallas_call(
        _a_out_kernel,
        out_shape=jax.ShapeDtypeStruct((n, D_MODEL), F32),
        grid=(n // tm,),
        in_specs=[row(D_MODEL), row(o.shape[1]), row(r.shape[1]), full(go), row(MEM_W),
                  full(wa), full(wm)],
        out_specs=row(D_MODEL),
        compiler_params=_params(("parallel",)),
        name="layer_a_out",
    )(x, o, r, go, om, wa, wm)


def _b_out_kernel(x_ref, o_ref, om_ref, wa_ref, wm_ref, y_ref):
    y_ref[...] = (x_ref[...] + _dot(o_ref[...].astype(BF16), wa_ref[...])
                  + _dot(om_ref[...].astype(BF16), wm_ref[...]))


def _b_out(x, o, om, wa, wm, tm):
    n = x.shape[0]
    row = lambda c: pl.BlockSpec((tm, c), lambda i: (i, 0))
    full = lambda a: pl.BlockSpec(a.shape, lambda i: (0,) * a.ndim)
    return pl.pallas_call(
        _b_out_kernel,
        out_shape=jax.ShapeDtypeStruct((n, D_MODEL), F32),
        grid=(n // tm,),
        in_specs=[row(D_MODEL), row(o.shape[1]), row(MEM_W), full(wa), full(wm)],
        out_specs=row(D_MODEL),
        compiler_params=_params(("parallel",)),
        name="layer_b_out",
    )(x, o, om, wa, wm)


L_C0 = 0
L_A0 = MLA_KV_RANK
L_B0 = L_A0 + LANE
L_COLS = L_B0 + LANE
KV_W = MLA_HEADS * HEAD_PAD


def _latent_kernel(x_ref, g_ref, wl_ref, gc_ref, ga_ref, gb_ref, cos_ref, sin_ref,
                   wuk_ref, gkn_ref, wuvt_ref, ones_ref,
                   c_ref, kpe_ref, kf_ref, vt_ref):
    hn = _rms(x_ref[...], g_ref[...]).astype(BF16)
    y = _dot(hn, wl_ref[...])
    c = _rms(y[:, L_C0:L_A0], gc_ref[...])
    c_ref[...] = c
    a = y[:, L_A0:L_B0]
    b = y[:, L_B0:L_COLS]
    r = lax.rsqrt(jnp.sum(a * a, axis=-1, keepdims=True) * (1.0 / MLA_DR) + EPS)
    kpe = a * r * ga_ref[...] * cos_ref[...] + b * r * gb_ref[...] * sin_ref[...]
    kpe_ref[...] = kpe
    cb = c.astype(BF16)
    kn = _dot(cb, wuk_ref[...])
    ones = ones_ref[...]
    gkn = gkn_ref[...]
    for h in range(MLA_HEADS):
        hs = slice(h * HEAD_PAD, (h + 1) * HEAD_PAD)
        knh = kn[:, hs]
        ms = _group_sum(knh * knh, ones) * (1.0 / MLA_DN)
        kf_ref[:, hs] = (knh * lax.rsqrt(ms + EPS) * gkn + kpe).astype(BF16)
    vt = _dot_nt(wuvt_ref[...], cb)
    slab_row = lax.broadcasted_iota(jnp.int32, vt.shape, 0) % HEAD_PAD
    vt_ref[...] = jnp.where(slab_row == V_ONE, 1.0, vt).astype(BF16)


def _latent(x, g, wl, gc, ga, gb, cos, sin, wuk, gkn, wuvt, ones_h, tm):
    n = x.shape[0]
    row = lambda c: pl.BlockSpec((tm, c), lambda i: (i, 0))
    full = lambda a: pl.BlockSpec(a.shape, lambda i: (0,) * a.ndim)
    return pl.pallas_call(
        _latent_kernel,
        out_shape=(jax.ShapeDtypeStruct((n, MLA_KV_RANK), F32), jax.ShapeDtypeStruct((n, LANE), F32),
                   jax.ShapeDtypeStruct((n, KV_W), BF16), jax.ShapeDtypeStruct((KV_W, n), BF16)),
        grid=(n // tm,),
        in_specs=[row(D_MODEL), full(g), full(wl), full(gc), full(ga), full(gb), row(LANE), row(LANE),
                  full(wuk), full(gkn), full(wuvt), full(ones_h)],
        out_specs=(row(MLA_KV_RANK), row(LANE), row(KV_W), pl.BlockSpec((KV_W, tm), lambda i: (0, i))),
        compiler_params=_params(("parallel",)),
        name="shared_latent",
    )(x, g, wl, gc, ga, gb, cos, sin, wuk, gkn, wuvt, ones_h)


B_COLS = MLA_Q_RANK + MEM_W


def _b_in_kernel(x_ref, g_ref, w_ref, gqa_ref, w1_ref, w2_ref, g1_ref, g2_ref, inv_ref,
                 cos_ref, sin_ref, gq_ref, onesh_ref, ones64_ref, q_ref, mq_ref):
    u = _rms(x_ref[...], g_ref[...]).astype(BF16)
    p = _dot(u, w_ref[...])
    cq = _rms(p[:, :MLA_Q_RANK], gqa_ref[...]).astype(BF16)
    q1 = _dot(cq, w1_ref[...])
    q2 = _dot(cq, w2_ref[...])
    onesh = onesh_ref[...]
    inv = inv_ref[...]
    c1 = g1_ref[...] * cos_ref[...]
    c2 = g2_ref[...] * sin_ref[...]
    for h in range(MLA_HEADS):
        hs = slice(h * HEAD_PAD, (h + 1) * HEAD_PAD)
        q1h = q1[:, hs]
        rs = lax.rsqrt(_group_sum(q1h * q1h, onesh) * inv + EPS)
        q_ref[:, hs] = ((q1h * c1 + q2[:, hs] * c2) * rs).astype(BF16)
    mq_ref[...] = _head_rms(p[:, MLA_Q_RANK:B_COLS], ones64_ref[...], MEM_DH, gq_ref[...])


def _b_in(x, g, w, gqa, w1, w2, g1, g2, inv, cos, sin, gq, ones_h, ones64, tm):
    n = x.shape[0]
    row = lambda c: pl.BlockSpec((tm, c), lambda i: (i, 0))
    full = lambda a: pl.BlockSpec(a.shape, lambda i: (0,) * a.ndim)
    return pl.pallas_call(
        _b_in_kernel,
        out_shape=(jax.ShapeDtypeStruct((n, KV_W), BF16), jax.ShapeDtypeStruct((n, MEM_W), F32)),
        grid=(n // tm,),
        in_specs=[row(D_MODEL), full(g), full(w), full(gqa), full(w1), full(w2), full(g1), full(g2),
                  full(inv), row(LANE), row(LANE), full(gq), full(ones_h), full(ones64)],
        out_specs=(row(KV_W), row(MEM_W)),
        compiler_params=_params(("parallel",)),
        name="layer_b_in",
    )(x, g, w, gqa, w1, w2, g1, g2, inv, cos, sin, gq, ones_h, ones64)


def _flash_kernel(q_ref, k_ref, vt_ref, o_ref, sa_sc, sb_sc, m_sc, acc_sc):
    qi = pl.program_id(2)
    tq = q_ref.shape[1]
    tk = sa_sc.shape[0]
    q = q_ref[0]

    def scores_t(kb):
        start = pl.multiple_of(kb * tk, tk)
        return _dot_nt(k_ref[0, pl.ds(start, tk), :], q)

    def accumulate(s, kb, diagonal):
        if diagonal is not None:
            key = diagonal + lax.broadcasted_iota(jnp.int32, (tk, tq), 0)
            qry = lax.broadcasted_iota(jnp.int32, (tk, tq), 1)
            s = jnp.where(key <= qry, s, NEG)
        m = m_sc[...]
        m_new = jnp.maximum(m, jnp.max(s, axis=0, keepdims=True))
        p = jnp.exp2(s - m_new).astype(BF16)
        start = pl.multiple_of(kb * tk, tk)
        acc_sc[...] = jnp.exp2(m - m_new) * acc_sc[...] + _dot(vt_ref[:, pl.ds(start, tk)], p)
        m_sc[...] = m_new

    m_sc[...] = jnp.full_like(m_sc, NEG)
    acc_sc[...] = jnp.zeros_like(acc_sc)
    sa_sc[...] = scores_t(0)

    def pair(i, carry):
        kb = 2 * i
        sb_sc[...] = scores_t(kb + 1)
        accumulate(sa_sc[...], kb, None)
        sa_sc[...] = scores_t(kb + 2)
        accumulate(sb_sc[...], kb + 1, None)
        return carry

    lax.fori_loop(0, qi, pair, 0)
    sb_sc[...] = scores_t(2 * qi + 1)
    accumulate(sa_sc[...], 2 * qi, 0)
    accumulate(sb_sc[...], 2 * qi + 1, tk)
    acc = acc_sc[...]
    o_ref[0] = (acc / acc[V_ONE:V_ONE + 1, :]).T.astype(o_ref.dtype)


def _flash(q, k, vt, tq):
    b, t, _ = q.shape
    tk = tq // 2
    qspec = pl.BlockSpec((1, tq, HEAD_PAD), lambda bi, h, i: (bi, i, h))
    return pl.pallas_call(
        _flash_kernel,
        out_shape=jax.ShapeDtypeStruct((b, t, KV_W), BF16),
        grid=(b, MLA_HEADS, t // tq),
        in_specs=[qspec,
                  pl.BlockSpec((1, t, HEAD_PAD), lambda bi, h, i: (bi, 0, h)),
                  pl.BlockSpec((HEAD_PAD, t), lambda bi, h, i: (h, bi))],
        out_specs=qspec,
        scratch_shapes=[pltpu.VMEM((tk, tq), F32), pltpu.VMEM((tk, tq), F32),
                        pltpu.VMEM((1, tq), F32), pltpu.VMEM((HEAD_PAD, tq), F32)],
        compiler_params=_params(("parallel", "parallel", "arbitrary")),
        name="prompt_latent_attention",
    )(q, k, vt)


PAGES_PER_STEP = 32
QROWS = MLA_HEADS * T_PAD
SUB = 512


def _absorb_kernel(q_ref, w_ref, o_ref):
    o_ref[0] = _dot(q_ref[...], w_ref[0]).astype(BF16)


def _absorb(q, wabs):
    n = q.shape[0]
    return pl.pallas_call(
        _absorb_kernel,
        out_shape=jax.ShapeDtypeStruct((MLA_HEADS, n, MLA_KV_RANK), BF16),
        grid=(MLA_HEADS,),
        in_specs=[pl.BlockSpec((n, HEAD_PAD), lambda h: (0, h)),
                  pl.BlockSpec((1, HEAD_PAD, MLA_KV_RANK), lambda h: (h, 0, 0))],
        out_specs=pl.BlockSpec((1, n, MLA_KV_RANK), lambda h: (h, 0, 0)),
        compiler_params=_params(("parallel",)),
        name="absorb_w_uk",
    )(q, wabs)


def _scores(kt, sn, sp):
    rows = []
    for h in range(MLA_HEADS):
        kth = kt[h * MLA_DN:(h + 1) * MLA_DN, :]
        r = lax.rsqrt(jnp.sum(kth * kth, axis=0, keepdims=True) * (1.0 / MLA_DN) + EPS)
        rows.append(sn[h * T_PAD:(h + 1) * T_PAD, :] * r)
    return jnp.concatenate(rows, axis=0) + sp


def _online_softmax(s, cb, m, l, acc):
    m_new = jnp.maximum(m, jnp.max(s, axis=-1, keepdims=True))
    alpha = jnp.exp2(m - m_new)
    p = jnp.exp2(s - m_new)
    l = alpha * l + jnp.sum(p, axis=-1, keepdims=True)
    acc = alpha * acc + _dot(p.astype(BF16), cb)
    return m_new, l, acc


def _sample_attn_kernel(pt_ref, ckv_hbm, kpe_hbm, qa_ref, qp_ref, wukt_ref, cn_ref, kn_ref, o_ref,
                        cbuf, kbuf, csem, ksem, kt_sc, sn_sc, sp_sc, cb_sc, *, n_chunks, ch_pages, sub):
    b = pl.program_id(0)
    nb = pl.num_programs(0)
    page = cbuf.shape[1] // ch_pages
    n_sub = ch_pages * page // sub
    qa = qa_ref[0]
    qp = qp_ref[0]
    wukt = wukt_ref[...]

    def chunk_copies(seq, ch, slot):
        copies = []
        for i in range(ch_pages):
            pg = pt_ref[seq, ch * ch_pages + i]
            rows = pl.ds(i * page, page)
            copies.append(pltpu.make_async_copy(ckv_hbm.at[pg], cbuf.at[slot, rows, :], csem.at[slot]))
            copies.append(pltpu.make_async_copy(kpe_hbm.at[pg], kbuf.at[slot, :, rows], ksem.at[slot]))
        return copies

    def stage1(slot, i, buf):
        cb = cbuf[slot, i * sub:(i + 1) * sub, :].astype(BF16)
        cb_sc[buf] = cb
        kt_sc[buf] = _dot_nt(wukt, cb)
        sn_sc[buf] = _dot_nt(qa, cb)
        sp_sc[buf] = _dot(qp, kbuf[slot, :, i * sub:(i + 1) * sub].astype(BF16))

    def stage2(buf, state):
        return _online_softmax(_scores(kt_sc[buf], sn_sc[buf], sp_sc[buf]), cb_sc[buf], *state)

    @pl.when(b == 0)
    def _():
        for c in chunk_copies(0, 0, 0):
            c.start()

    cn = cn_ref[0].astype(BF16)
    s = _scores(_dot_nt(wukt, cn), _dot_nt(qa, cn), _dot(qp, kn_ref[0].astype(BF16)))
    t_q = lax.broadcasted_iota(jnp.int32, s.shape, 0) % T_PAD
    t_k = lax.broadcasted_iota(jnp.int32, s.shape, 1)
    init = (jnp.full((QROWS, 1), NEG, F32), jnp.zeros((QROWS, 1), F32), jnp.zeros((QROWS, MLA_KV_RANK), F32))
    state = _online_softmax(jnp.where(t_k <= t_q, s, NEG), cn, *init)

    def chunk(ch, slot, state):
        for c in chunk_copies(b, ch, slot):
            c.wait()
        in_seq = ch + 1 < n_chunks
        for c in chunk_copies(jnp.where(in_seq, b, (b + 1) % nb), jnp.where(in_seq, ch + 1, 0), 1 - slot):
            c.start()
        stage1(slot, 0, 0)
        for i in range(n_sub):
            if i + 1 < n_sub:
                stage1(slot, i + 1, (i + 1) % 2)
            state = stage2(i % 2, state)
        return state

    def chunk_pair(pi, state):
        return chunk(2 * pi + 1, 1, chunk(2 * pi, 0, state))

    _, l, acc = lax.fori_loop(0, n_chunks // 2, chunk_pair, state)
    @pl.when(b == nb - 1)
    def _():
        for c in chunk_copies(0, 0, 0):
            c.wait()

    o_ref[0] = acc / l


def _sample_attn(page_table, cache_ckv, cache_kpe_t, qa, qp, wukt, c_new, k_new_t):
    nb, n_pages = page_table.shape
    page = cache_ckv.shape[1]
    ch_pages = math.gcd(n_pages // 2, PAGES_PER_STEP)
    n_chunks = n_pages // ch_pages
    tokens = ch_pages * page
    sub = min(SUB, tokens // 2)
    assert tokens % (2 * sub) == 0 and n_chunks % 2 == 0
    per_b = lambda r, w: pl.BlockSpec((1, r, w), lambda b, pt: (b, 0, 0))
    hbm = pl.BlockSpec(memory_space=pl.ANY)
    return pl.pallas_call(
        functools.partial(_sample_attn_kernel, n_chunks=n_chunks, ch_pages=ch_pages, sub=sub),
        out_shape=jax.ShapeDtypeStruct((nb, QROWS, MLA_KV_RANK), F32),
        grid_spec=pltpu.PrefetchScalarGridSpec(
            num_scalar_prefetch=1,
            grid=(nb,),
            in_specs=[hbm, hbm, per_b(QROWS, MLA_KV_RANK), per_b(QROWS, MLA_DR),
                      pl.BlockSpec(wukt.shape, lambda b, pt: (0, 0)),
                      per_b(c_new.shape[1], MLA_KV_RANK), per_b(MLA_DR, k_new_t.shape[2])],
            out_specs=per_b(QROWS, MLA_KV_RANK),
            scratch_shapes=[pltpu.VMEM((2, tokens, MLA_KV_RANK), F32),
                            pltpu.VMEM((2, MLA_DR, tokens), F32),
                            pltpu.SemaphoreType.DMA((2,)), pltpu.SemaphoreType.DMA((2,)),
                            pltpu.VMEM((2, MLA_HEADS * MLA_DN, sub), F32),
                            pltpu.VMEM((2, QROWS, sub), F32),
                            pltpu.VMEM((2, QROWS, sub), F32),
                            pltpu.VMEM((2, sub, MLA_KV_RANK), BF16)],
        ),
        compiler_params=_params(("arbitrary",)),
        name="sample_latent_attention",
    )(page_table, cache_ckv, cache_kpe_t, qa, qp, wukt, c_new, k_new_t)


def _uv_kernel(x_ref, w_ref, o_ref):
    o_ref[0] = _dot(x_ref[0].astype(BF16), w_ref[0])


def _uv_project(ctx, wuv):
    _, n, _ = ctx.shape
    return pl.pallas_call(
        _uv_kernel,
        out_shape=jax.ShapeDtypeStruct((MLA_HEADS, n, HEAD_PAD), F32),
        grid=(MLA_HEADS,),
        in_specs=[pl.BlockSpec((1, n, MLA_KV_RANK), lambda h: (h, 0, 0)),
                  pl.BlockSpec((1, MLA_KV_RANK, HEAD_PAD), lambda h: (h, 0, 0))],
        out_specs=pl.BlockSpec((1, n, HEAD_PAD), lambda h: (h, 0, 0)),
        compiler_params=_params(("parallel",)),
        name="apply_w_uv",
    )(ctx, wuv)


def _memkv_kernel(m_ref, g_ref, wkt_ref, wvt_ref, gk_ref, ones_ref, kt_ref, vt_ref):
    mn = _rms(m_ref[0], g_ref[0]).astype(BF16)
    kt = _dot_nt(wkt_ref[0], mn)
    hi, mid, _ = _split3(kt * kt)
    ones = ones_ref[...]
    ms = (_dot(ones, hi) + _dot(ones, mid)) * (1.0 / MEM_DH)
    kt_ref[0, 0] = kt * lax.rsqrt(ms + EPS) * gk_ref[0]
    vt_ref[0, 0] = _dot_nt(wvt_ref[0], mn)


def _memkv(mem, g, wkt, wvt, gk, ones64):
    nb, m, _ = mem.shape
    nl = wkt.shape[0]
    lay = lambda a: pl.BlockSpec((1,) + a.shape[1:], lambda l, b: (l,) + (0,) * (a.ndim - 1))
    out = pl.BlockSpec((1, 1, MEM_W, m), lambda l, b: (l, b, 0, 0))
    shape = jax.ShapeDtypeStruct((nl, nb, MEM_W, m), F32)
    return pl.pallas_call(
        _memkv_kernel,
        out_shape=(shape, shape),
        grid=(nl, nb),
        in_specs=[pl.BlockSpec((1, m, D_MODEL), lambda l, b: (b, 0, 0)), lay(g), lay(wkt), lay(wvt), lay(gk),
                  pl.BlockSpec(ones64.shape, lambda l, b: (0, 0))],
        out_specs=(out, out),
        compiler_params=_params(("parallel", "parallel")),
        name="prompt_mem_kv",
    )(mem, g, wkt, wvt, gk, ones64)


def _pad_heads(w, heads, size, padded, axis=-1):
    axis = axis % w.ndim
    shape = w.shape[:axis] + (heads, size) + w.shape[axis + 1:]
    w = w.reshape(shape)
    pad = [(0, 0)] * w.ndim
    pad[axis + 1] = (0, padded - size)
    w = jnp.pad(w, pad)
    return w.reshape(w.shape[:axis] + (heads * padded,) + w.shape[axis + 2:])


def _block_ones(n, sizes):
    gid = []
    g = 0
    while len(gid) < n:
        for s in sizes:
            gid += [g] * s
            g += 1
    gid = jnp.asarray(gid[:n], jnp.int32)
    return (gid[:, None] == gid[None, :]).astype(BF16)


def _rope_tables(pos):
    half = MLA_DR // 2
    inv = ROPE_THETA ** (-jnp.arange(half, dtype=F32) / half)
    ang = pos.astype(F32)[:, None] * inv
    cos, sin = jnp.cos(ang), jnp.sin(ang)
    z = jnp.zeros((pos.shape[0], MLA_DN), F32)
    z2 = jnp.zeros((pos.shape[0], HEAD_PAD - MLA_DN - MLA_DR), F32)
    return (jnp.concatenate([z + 1.0, cos, cos, z2], axis=-1), jnp.concatenate([z, -sin, sin, z2], axis=-1))


def _swap_halves(w):
    half = w.shape[-1] // 2
    return jnp.concatenate([w[..., half:], w[..., :half]], axis=-1)


def _rope_slab(w):
    pad = [(0, 0)] * (w.ndim - 1) + [(MLA_DN, HEAD_PAD - MLA_DN - MLA_DR)]
    return jnp.pad(w, pad)


def _tile_for(n, pref):
    t = min(n, pref)
    while n % t:
        t //= 2
    return t


def kernel(x_prompt, x_sample, mem_prompt, cache_mem_k, cache_mem_v, state_gla, cache_ckv, cache_kpe, page_table,
           ffn1_norm, ffn1_w_gate, ffn1_w_up, ffn1_w_down, ffn2_norm, ffn2_w_gate, ffn2_w_up, ffn2_w_down,
           mix_norm, w_out, mem_norm, w_mem_k, w_mem_v, mem_k_norm, mem_q_norm,
           gla_w_in, gla_w_gate2, gla_b_gate, gla_o_norm,
           mla_w_in, mla_q_a_norm, mla_w_uq, mla_q_nope_norm, mla_q_pe_norm,
           kv_norm, kv_w_dkv, kv_ckv_norm, kv_w_kr, kv_kpe_norm, kv_w_uk, kv_k_nope_norm, kv_w_uv):
    bp, tp, _ = x_prompt.shape
    bs, ts, _ = x_sample.shape
    n_mem = mem_prompt.shape[1]
    past = page_table.shape[1] * cache_ckv.shape[1]
    row = lambda a: a.reshape(1, -1).astype(F32)
    bf = lambda a: a.astype(BF16)

    ffn = [[(row(n[l]), bf(g[l]), bf(u[l]), bf(d[l])) for l in range(2)]
           for n, g, u, d in ((ffn1_norm, ffn1_w_gate, ffn1_w_up, ffn1_w_down),
                              (ffn2_norm, ffn2_w_gate, ffn2_w_up, ffn2_w_down))]
    ones64 = _block_ones(MEM_W, [MEM_DH])
    ones_head = _block_ones(HEAD_PAD, [MLA_DN, MLA_DR, HEAD_PAD - MLA_DN - MLA_DR])
    gmq = [row(jnp.tile(mem_q_norm[l], MEM_HEADS)) for l in range(2)]

    wi = gla_w_in[0]
    o_q, o_k, o_v, o_g, o_r, o_m = 0, GLA_QK, 2 * GLA_QK, 2 * GLA_QK + GLA_V, 2 * GLA_QK + GLA_V + GLA_GATE_RANK, \
        2 * GLA_QK + 2 * GLA_V + GLA_GATE_RANK
    wa_in = bf(jnp.concatenate([
        _pad_heads(wi[:, o_q:o_k], GLA_HEADS, GLA_DK, DK_PAD),
        _pad_heads(wi[:, o_k:o_v], GLA_HEADS, GLA_DK, DK_PAD),
        _pad_heads(wi[:, o_v:o_g], GLA_HEADS, GLA_DV, DV_PAD),
        _pad_heads(wi[:, o_r:o_m], GLA_HEADS, GLA_DV, DV_PAD),
        wi[:, o_m:],
        jnp.pad(wi[:, o_g:o_r], ((0, 0), (0, LANE - GLA_GATE_RANK))),
    ], axis=1))
    w_gate2 = bf(jnp.pad(_pad_heads(gla_w_gate2[0], GLA_HEADS, GLA_DK, DK_PAD),
                         ((0, LANE - GLA_GATE_RANK), (0, 0))))
    b_gate = row(_pad_heads(gla_b_gate[0], GLA_HEADS, GLA_DK, DK_PAD))
    g_o = row(jnp.pad(gla_o_norm[0], (0, DV_PAD - GLA_DV)))
    g_o = jnp.tile(g_o, (1, GLA_HEADS))
    wa_out_main = bf(_pad_heads(w_out[0][:GLA_V], GLA_HEADS, GLA_DV, DV_PAD, axis=0))
    wa_out_mem = bf(w_out[0][GLA_V:])

    kr_slab = _rope_slab(kv_w_kr)
    w_lat = bf(jnp.concatenate([kv_w_dkv, kr_slab, _rope_slab(_swap_halves(kv_w_kr))], axis=1))
    g_kpe_a = row(_rope_slab(kv_kpe_norm))
    g_kpe_b = row(_rope_slab(_swap_halves(kv_kpe_norm)))
    w_uk_pad = bf(_pad_heads(kv_w_uk, MLA_HEADS, MLA_DN, HEAD_PAD))
    w_uv_t = bf(_pad_heads(kv_w_uv, MLA_HEADS, MLA_DV, HEAD_PAD).T)
    g_kn_slab = row(jnp.pad(kv_k_nope_norm, (0, HEAD_PAD - MLA_DN)))

    scale = (MLA_DN + MLA_DR) ** -0.5 * LOG2E
    wuq = mla_w_uq[0].reshape(MLA_Q_RANK, MLA_HEADS, MLA_DN + MLA_DR)
    w_q1 = bf(jnp.pad(wuq, ((0, 0), (0, 0), (0, HEAD_PAD - MLA_DN - MLA_DR))).reshape(MLA_Q_RANK, KV_W))
    w_q2 = bf(_rope_slab(_swap_halves(wuq[..., MLA_DN:])).reshape(MLA_Q_RANK, KV_W))
    g_q1 = row(jnp.concatenate([mla_q_nope_norm[0], mla_q_pe_norm[0],
                                jnp.zeros((HEAD_PAD - MLA_DN - MLA_DR,), F32)])) * scale
    g_q2 = row(_rope_slab(_swap_halves(mla_q_pe_norm[0]))) * scale
    inv_sizes = row(jnp.concatenate([jnp.full((MLA_DN,), 1.0 / MLA_DN, F32),
                                     jnp.full((HEAD_PAD - MLA_DN,), 1.0 / MLA_DR, F32)]))
    wb_in = bf(mla_w_in[0])
    wb_out_main = bf(_pad_heads(w_out[1][:MLA_HEADS * MLA_DV], MLA_HEADS, MLA_DV, HEAD_PAD, axis=0))
    wb_out_mem = bf(w_out[1][MLA_HEADS * MLA_DV:])
    wuk_h = kv_w_uk.reshape(MLA_KV_RANK, MLA_HEADS, MLA_DN).transpose(1, 2, 0)
    w_abs = bf(jnp.pad(wuk_h * kv_k_nope_norm[None, :, None], ((0, 0), (0, HEAD_PAD - MLA_DN), (0, 0))))
    w_uk_t = bf(kv_w_uk.T)
    w_uv_h = bf(jnp.pad(kv_w_uv.reshape(MLA_KV_RANK, MLA_HEADS, MLA_DV).transpose(1, 0, 2),
                        ((0, 0), (0, 0), (0, HEAD_PAD - MLA_DV))))

    g_mem_k = jnp.broadcast_to(jnp.tile(mem_k_norm, (1, MEM_HEADS))[:, :, None], (2, MEM_W, n_mem))
    mem_kt_p, mem_vt_p = _memkv(mem_prompt, mem_norm.reshape(2, 1, D_MODEL),
                                bf(w_mem_k.swapaxes(1, 2)), bf(w_mem_v.swapaxes(1, 2)), g_mem_k, ones64)

    def mem_out(a):
        return a.reshape(2, bp, MEM_HEADS, MEM_DH, n_mem).transpose(0, 1, 4, 2, 3)

    def trunk(x, nb, t, pos, mem_k, mem_v, s0, attend):
        n = nb * t
        tm = _tile_for(n, FFN_ROWS)
        tp_ = _tile_for(n, PROJ_ROWS)
        ta = _tile_for(n, A_IN_ROWS)
        seq_pad = (-t) % T_PAD
        tpad = t + seq_pad

        def to_seq(a):
            a = a.reshape(nb, t, a.shape[-1])
            return jnp.pad(a, ((0, 0), (0, seq_pad), (0, 0))) if seq_pad else a

        def from_seq(a):
            return a[:, :t].reshape(n, a.shape[-1])

        cos, sin = _rope_tables(pos)
        cos = jnp.tile(cos, (nb, 1))
        sin = jnp.tile(sin, (nb, 1))

        def mem_part(mq, l):
            return from_seq(_mem_attn(to_seq(mq), mem_k, mem_v, l, _tile_for(tpad, 1024)))

        x = _ffn_half(x, *ffn[0][0], tm)
        q, k, v, la, r, mq = _a_in(x, row(mix_norm[0]), wa_in, w_gate2, b_gate, gmq[0], ones64, ta)
        s0t = jnp.pad(s0.transpose(0, 1, 3, 2), ((0, 0), (0, 0), (0, DV_PAD - GLA_DV), (0, DK_PAD - GLA_DK)))
        o, st = _gla(to_seq(q), to_seq(k), to_seq(v), to_seq(la), s0t, math.gcd(tpad, GLA_CHUNK))
        gla_state = st[:, :, :GLA_DV, :GLA_DK].transpose(0, 1, 3, 2)
        x = _a_out(x, from_seq(o), r, g_o, mem_part(mq, 0), wa_out_main, wa_out_mem, tp_)
        x = _ffn_half(x, *ffn[1][0], tm)
        c, kpe, k_full, v_full = _latent(x, row(kv_norm), w_lat, row(kv_ckv_norm), g_kpe_a, g_kpe_b, cos, sin,
                                         w_uk_pad, g_kn_slab, w_uv_t, ones_head, tp_)
        kpe = kpe[:, MLA_DN:MLA_DN + MLA_DR]
        x = _ffn_half(x, *ffn[0][1], tm)
        q_full, mq = _b_in(x, row(mix_norm[1]), wb_in, row(mla_q_a_norm[0]), w_q1, w_q2, g_q1, g_q2, inv_sizes,
                           cos, sin, gmq[1], ones_head, ones64, tp_)
        o_main = attend(q_full, k_full, v_full, c, kpe)
        x = _b_out(x, o_main, mem_part(mq, 1), wb_out_main, wb_out_mem, tp_)
        x = _ffn_half(x, *ffn[1][1], tm)
        return x, gla_state, c, kpe

    def attend_prompt(q_full, k_full, v_full, c, kpe):
        sh = (bp, tp, KV_W)
        o = _flash(q_full.reshape(sh), k_full.reshape(sh), v_full, _tile_for(tp, FLASH_ROWS))
        return o.reshape(bp * tp, KV_W)

    def attend_sample(q_full, k_full, v_full, c, kpe):
        n = bs * ts
        qa = _absorb(q_full, w_abs)

        def rows(a):
            a = a.reshape(MLA_HEADS, bs, ts, a.shape[-1]).transpose(1, 0, 2, 3)
            a = jnp.pad(a, ((0, 0), (0, 0), (0, T_PAD - ts), (0, 0)))
            return a.reshape(bs, QROWS, a.shape[-1])

        qp = q_full.reshape(n, MLA_HEADS, HEAD_PAD)[:, :, MLA_DN:MLA_DN + MLA_DR].transpose(1, 0, 2)
        page = cache_ckv.shape[1]
        c_new = jnp.pad(c.reshape(bs, ts, MLA_KV_RANK), ((0, 0), (0, page - ts), (0, 0)))
        k_new_t = jnp.pad(kpe.reshape(bs, ts, MLA_DR), ((0, 0), (0, page - ts), (0, 0))).swapaxes(1, 2)
        ctx = _sample_attn(page_table, cache_ckv, cache_kpe.swapaxes(1, 2), rows(qa), rows(qp), w_uk_t,
                           c_new, k_new_t)
        ctx = ctx.reshape(bs, MLA_HEADS, T_PAD, MLA_KV_RANK).transpose(1, 0, 2, 3)
        o = _uv_project(ctx.reshape(MLA_HEADS, bs * T_PAD, MLA_KV_RANK), w_uv_h)
        o = o.reshape(MLA_HEADS, bs, T_PAD, HEAD_PAD)[:, :, :ts].transpose(1, 2, 0, 3)
        return o.reshape(n, KV_W)

    mk_p = mem_kt_p.reshape(2 * bp, MEM_W, n_mem)
    mv_p = mem_vt_p.reshape(2 * bp, MEM_W, n_mem)
    s0_p = jnp.zeros((bp, GLA_HEADS, GLA_DK, GLA_DV), F32)
    y_p, st_p, c_p, kpe_p = trunk(x_prompt.reshape(bp * tp, D_MODEL), bp, tp, jnp.arange(tp), mk_p, mv_p,
                                  s0_p, attend_prompt)

    def mem_in(a):
        return a.transpose(0, 1, 3, 4, 2).reshape(2 * bs, MEM_W, n_mem)

    mk_s = mem_in(cache_mem_k)
    mv_s = mem_in(cache_mem_v)
    y_s, st_s, c_s, kpe_s = trunk(x_sample.reshape(bs * ts, D_MODEL), bs, ts, past + jnp.arange(ts), mk_s, mv_s,
                                  state_gla[0], attend_sample)

    return (y_p.reshape(bp, tp, D_MODEL), y_s.reshape(bs, ts, D_MODEL),
            st_p[None], st_s[None],
            c_p.reshape(bp, tp, MLA_KV_RANK), kpe_p.reshape(bp, tp, MLA_DR),
            c_s.reshape(bs, ts, MLA_KV_RANK), kpe_s.reshape(bs, ts, MLA_DR),
            mem_out(mem_kt_p), mem_out(mem_vt_p))
```

```python
import functools
import math

import jax
import jax.numpy as jnp
from jax import lax
from jax.experimental import pallas as pl
from jax.experimental.pallas import tpu as pltpu

F32 = jnp.float32
BF16 = jnp.bfloat16

D_MODEL = 1024
D_FF = 2816
GLA_HEADS = 4
GLA_DK = 96
GLA_DV = 192
GLA_QK = GLA_HEADS * GLA_DK
GLA_V = GLA_HEADS * GLA_DV
GLA_GATE_RANK = 16
GLA_GATE_TEMP = 16.0
MEM_HEADS = 4
MEM_DH = 64
MEM_W = MEM_HEADS * MEM_DH
MLA_HEADS = 12
MLA_DN = 64
MLA_DR = 32
MLA_DV = 64
MLA_KV_RANK = 256
MLA_Q_RANK = 384
ROPE_THETA = 10000.0
EPS = 1e-6

LANE = 128
DK_PAD = LANE
DV_PAD = 2 * LANE
HEAD_PAD = LANE
V_ONE = MLA_DV
LOG2E = math.log2(math.e)
GLA_CHUNK = 64
GLA_SUB = 8
GLA_SEQS = 2
T_PAD = 8
NEG = -1e30
VMEM_LIMIT = 56 * 1024 * 1024
FFN_ROWS = 1024
FLASH_ROWS = 1024
FFN_COLS = 256
PROJ_ROWS = 512
A_IN_ROWS = 256

NT_DIMS = (((1,), (1,)), ((), ()))
TN_DIMS = (((0,), (0,)), ((), ()))


def _dot(a, b):
    return jnp.dot(a, b, preferred_element_type=F32)


def _dot_nt(a, b):
    return lax.dot_general(a, b, NT_DIMS, preferred_element_type=F32)


def _dot_tn(a, b):
    return lax.dot_general(a, b, TN_DIMS, preferred_element_type=F32)


def _rms(x, g):
    return x * lax.rsqrt(jnp.mean(x * x, axis=-1, keepdims=True) + EPS) * g


def _split3(x):
    hi = x.astype(BF16)
    r1 = x - hi.astype(F32)
    mid = r1.astype(BF16)
    lo = (r1 - mid.astype(F32)).astype(BF16)
    return hi, mid, lo


def _group_sum(x, ones_bd):
    hi, mid, _ = _split3(x)
    return _dot(hi, ones_bd) + _dot(mid, ones_bd)


def _params(sem):
    return pltpu.CompilerParams(dimension_semantics=sem, vmem_limit_bytes=VMEM_LIMIT)


def _ffn_kernel(x_ref, g_ref, wg_ref, wu_ref, wd_ref, o_ref, act_sc):
    x = x_ref[...]
    h = _rms(x, g_ref[...]).astype(BF16)
    for f in range(D_FF // FFN_COLS):
        cols = slice(f * FFN_COLS, (f + 1) * FFN_COLS)
        gate = _dot(h, wg_ref[:, cols])
        up = _dot(h, wu_ref[:, cols])
        act_sc[:, cols] = (gate * jax.nn.sigmoid(gate) * up).astype(BF16)
    o_ref[...] = x + 0.5 * _dot(act_sc[...], wd_ref[...])


def _ffn_half(x, g, wg, wu, wd, tm):
    n = x.shape[0]
    resident = lambda a: pl.BlockSpec(a.shape, lambda i: (0,) * a.ndim, pipeline_mode=pl.Buffered(1))
    return pl.pallas_call(
        _ffn_kernel,
        out_shape=jax.ShapeDtypeStruct((n, D_MODEL), F32),
        grid=(n // tm,),
        in_specs=[pl.BlockSpec((tm, D_MODEL), lambda i: (i, 0)), resident(g), resident(wg), resident(wu),
                  resident(wd)],
        out_specs=pl.BlockSpec((tm, D_MODEL), lambda i: (i, 0)),
        scratch_shapes=[pltpu.VMEM((tm, D_FF), BF16)],
        compiler_params=_params(("parallel",)),
        name="ffn_half",
    )(x, g, wg, wu, wd)


A_Q0 = 0
A_K0 = A_Q0 + GLA_HEADS * DK_PAD
A_V0 = A_K0 + GLA_HEADS * DK_PAD
A_R0 = A_V0 + GLA_HEADS * DV_PAD
A_M0 = A_R0 + GLA_HEADS * DV_PAD
A_G0 = A_M0 + MEM_W
A_COLS = A_G0 + LANE


def _head_rms(x, ones_bd, size, g):
    return x * lax.rsqrt(_group_sum(x * x, ones_bd) * (1.0 / size) + EPS) * g


def _a_in_kernel(x_ref, g_ref, w_ref, w2_ref, b2_ref, gq_ref, ones_ref,
                 q_ref, k_ref, v_ref, la_ref, r_ref, mq_ref):
    u = _rms(x_ref[...], g_ref[...]).astype(BF16)
    p = _dot(u, w_ref[...])
    q_ref[...] = p[:, A_Q0:A_K0] * (GLA_DK ** -0.5)
    k_ref[...] = p[:, A_K0:A_V0]
    v_ref[...] = p[:, A_V0:A_R0].astype(BF16)
    r = p[:, A_R0:A_M0]
    r_ref[...] = (r * jax.nn.sigmoid(r)).astype(BF16)
    mq_ref[...] = _head_rms(p[:, A_M0:A_G0], ones_ref[...], MEM_DH, gq_ref[...])
    z = _dot(p[:, A_G0:A_COLS].astype(BF16), w2_ref[...]) + b2_ref[...]
    log_sig = jnp.minimum(z, 0.0) - jnp.log1p(jnp.exp(-jnp.abs(z)))
    la_ref[...] = log_sig * (1.0 / GLA_GATE_TEMP)


def _a_in(x, g, w, w2, b2, gq, ones64, tm):
    n = x.shape[0]
    qk = GLA_HEADS * DK_PAD
    vv = GLA_HEADS * DV_PAD
    row = lambda c: pl.BlockSpec((tm, c), lambda i: (i, 0))
    full = lambda a: pl.BlockSpec(a.shape, lambda i: (0,) * a.ndim)
    return pl.pallas_call(
        _a_in_kernel,
        out_shape=(jax.ShapeDtypeStruct((n, qk), F32), jax.ShapeDtypeStruct((n, qk), F32),
                   jax.ShapeDtypeStruct((n, vv), BF16), jax.ShapeDtypeStruct((n, qk), F32),
                   jax.ShapeDtypeStruct((n, vv), BF16), jax.ShapeDtypeStruct((n, MEM_W), F32)),
        grid=(n // tm,),
        in_specs=[row(D_MODEL), full(g), full(w), full(w2), full(b2), full(gq), full(ones64)],
        out_specs=(row(qk), row(qk), row(vv), row(qk), row(vv), row(MEM_W)),
        compiler_params=_params(("parallel",)),
        name="layer_a_in",
    )(x, g, w, w2, b2, gq, ones64)


def _gla_kernel(q_ref, k_ref, v_ref, la_ref, s0_ref, o_ref, s_out_ref, st_sc):
    c = pl.program_id(1)
    nseq, chunk, _ = q_ref.shape
    sub = min(GLA_SUB, chunk)
    n_sc = chunk // sub

    @pl.when(c == 0)
    def _():
        st_sc[...] = s0_ref[...]

    t_idx = lax.broadcasted_iota(jnp.int32, (chunk, chunk), 0)
    s_idx = lax.broadcasted_iota(jnp.int32, (chunk, chunk), 1)
    tri = jnp.where(s_idx <= t_idx, 1.0, 0.0).astype(BF16)
    later = [(s_idx // sub == j) & (t_idx // sub > j) for j in range(n_sc - 1)]
    diag = [(s_idx == t_idx - d) & (t_idx % sub >= d) for d in range(sub)]
    for g in range(nseq):
        hi, mid, lo = _split3(la_ref[g])
        a_cum = _dot(tri, hi) + _dot(tri, mid) + _dot(tri, lo)
        a_last = a_cum[chunk - 1:chunk, :]
        a_end = jnp.concatenate(
            [jnp.broadcast_to(a_cum[(j + 1) * sub - 1:(j + 1) * sub, :], (sub, a_cum.shape[1]))
             for j in range(n_sc)], axis=0)
        e_q = jnp.exp(a_cum)
        e_kend = jnp.exp(a_end - a_cum)
        e_kd = jnp.exp(a_last - a_cum)
        e_last = jnp.exp(a_last)
        q = q_ref[g]
        k = k_ref[g]
        v = v_ref[g]
        for h in range(GLA_HEADS):
            ks = slice(h * DK_PAD, (h + 1) * DK_PAD)
            vs = slice(h * DV_PAD, (h + 1) * DV_PAD)
            a = a_cum[:, ks]
            qf = q[:, ks]
            kf = k[:, ks]
            k_end = (kf * e_kend[:, ks]).astype(BF16)
            scores = jnp.zeros((chunk, chunk), F32)
            for j in range(n_sc - 1):
                e_j = a[(j + 1) * sub - 1:(j + 1) * sub, :]
                q_j = (qf * jnp.exp(jnp.minimum(a - e_j, 0.0))).astype(BF16)
                scores = scores + jnp.where(later[j], _dot_nt(q_j, k_end), 0.0)
            for d in range(sub):
                k_d = kf if d == 0 else pltpu.roll(kf, d, 0)
                a_d = a if d == 0 else pltpu.roll(a, d, 0)
                pair = jnp.sum(qf * k_d * jnp.exp(jnp.minimum(a - a_d, 0.0)), axis=-1, keepdims=True)
                scores = scores + jnp.where(diag[d], pair, 0.0)
            qh = (qf * e_q[:, ks]).astype(BF16)
            kd = (kf * e_kd[:, ks]).astype(BF16)
            vh = v[:, vs]
            st = st_sc[g, h]
            o_ref[g, :, vs] = _dot(scores.astype(BF16), vh) + _dot_nt(qh, st.astype(BF16))
            st_sc[g, h] = st * e_last[:, ks] + _dot_tn(vh, kd)

    @pl.when(c == pl.num_programs(1) - 1)
    def _():
        s_out_ref[...] = st_sc[...]


def _gla(q, k, v, la, s0t, chunk):
    b, t, _ = q.shape
    qk = GLA_HEADS * DK_PAD
    vv = GLA_HEADS * DV_PAD
    nseq = math.gcd(b, GLA_SEQS)
    seq = lambda c: pl.BlockSpec((nseq, chunk, c), lambda i, j: (i, j, 0))
    st = pl.BlockSpec((nseq, GLA_HEADS, DV_PAD, DK_PAD), lambda i, j: (i, 0, 0, 0))
    return pl.pallas_call(
        _gla_kernel,
        out_shape=(jax.ShapeDtypeStruct((b, t, vv), F32),
                   jax.ShapeDtypeStruct((b, GLA_HEADS, DV_PAD, DK_PAD), F32)),
        grid=(b // nseq, t // chunk),
        in_specs=[seq(qk), seq(qk), seq(vv), seq(qk), st],
        out_specs=(seq(vv), st),
        scratch_shapes=[pltpu.VMEM((nseq, GLA_HEADS, DV_PAD, DK_PAD), F32)],
        compiler_params=_params(("parallel", "arbitrary")),
        name="gla_recurrence",
    )(q, k, v, la, s0t)


def _mem_attn_kernel(q_ref, kt_ref, vt_ref, o_ref):
    q = q_ref[0]
    kt = kt_ref[0].astype(BF16)
    vt = vt_ref[0].astype(BF16)
    head = lax.broadcasted_iota(jnp.int32, (1, MEM_W), 1) // MEM_DH
    acc = jnp.zeros(q.shape, F32)
    for h in range(MEM_HEADS):
        sel = head == h
        qh = jnp.where(sel, q, 0.0).astype(BF16)
        s = _dot(qh, kt) * (MEM_DH ** -0.5)
        p = jnp.exp(s - jnp.max(s, axis=-1, keepdims=True))
        pv = _dot_nt(p.astype(BF16), vt) / jnp.sum(p, axis=-1, keepdims=True)
        acc = acc + jnp.where(sel, pv, 0.0)
    o_ref[0] = acc


def _mem_attn(q, mkt, mvt, layer, tq):
    b, t, _ = q.shape
    m = mkt.shape[2]
    base = layer * b
    return pl.pallas_call(
        _mem_attn_kernel,
        out_shape=jax.ShapeDtypeStruct((b, t, MEM_W), F32),
        grid=(b, t // tq),
        in_specs=[pl.BlockSpec((1, tq, MEM_W), lambda i, j: (i, j, 0)),
                  pl.BlockSpec((1, MEM_W, m), lambda i, j: (base + i, 0, 0)),
                  pl.BlockSpec((1, MEM_W, m), lambda i, j: (base + i, 0, 0))],
        out_specs=pl.BlockSpec((1, tq, MEM_W), lambda i, j: (i, j, 0)),
        compiler_params=_params(("parallel", "parallel")),
        name="mem_attention",
    )(q, mkt, mvt)


def _a_out_kernel(x_ref, o_ref, r_ref, go_ref, om_ref, wa_ref, wm_ref, y_ref):
    o = o_ref[...]
    parts = []
    for h in range(GLA_HEADS):
        oh = o[:, h * DV_PAD:(h + 1) * DV_PAD]
        ms = jnp.sum(oh * oh, axis=-1, keepdims=True) * (1.0 / GLA_DV)
        parts.append(oh * lax.rsqrt(ms + EPS))
    on = jnp.concatenate(parts, axis=-1) * go_ref[...]
    main = (on * r_ref[...]).astype(BF16)
    y_ref[...] = (x_ref[...] + _dot(main, wa_ref[...])
                  + _dot(om_ref[...].astype(BF16), wm_ref[...]))


def _a_out(x, o, r, go, om, wa, wm, tm):
    n = x.shape[0]
    row = lambda c: pl.BlockSpec((tm, c), lambda i: (i, 0))
    full = lambda a: pl.BlockSpec(a.shape, lambda i: (0,) * a.ndim)
    return pl.pallas_call(
        _a_out_kernel,
        out_shape=jax.ShapeDtypeStruct((n, D_MODEL), F32),
        grid=(n // tm,),
        in_specs=[row(D_MODEL), row(o.shape[1]), row(r.shape[1]), full(go), row(MEM_W),
                  full(wa), full(wm)],
        out_specs=row(D_MODEL),
        compiler_params=_params(("parallel",)),
        name="layer_a_out",
    )(x, o, r, go, om, wa, wm)


def _b_out_kernel(x_ref, o_ref, om_ref, wa_ref, wm_ref, y_ref):
    y_ref[...] = (x_ref[...] + _dot(o_ref[...].astype(BF16), wa_ref[...])
                  + _dot(om_ref[...].astype(BF16), wm_ref[...]))


def _b_out(x, o, om, wa, wm, tm):
    n = x.shape[0]
    row = lambda c: pl.BlockSpec((tm, c), lambda i: (i, 0))
    full = lambda a: pl.BlockSpec(a.shape, lambda i: (0,) * a.ndim)
    return pl.pallas_call(
        _b_out_kernel,
        out_shape=jax.ShapeDtypeStruct((n, D_MODEL), F32),
        grid=(n // tm,),
        in_specs=[row(D_MODEL), row(o.shape[1]), row(MEM_W), full(wa), full(wm)],
        out_specs=row(D_MODEL),
        compiler_params=_params(("parallel",)),
        name="layer_b_out",
    )(x, o, om, wa, wm)


L_C0 = 0
L_A0 = MLA_KV_RANK
L_B0 = L_A0 + LANE
L_COLS = L_B0 + LANE
KV_W = MLA_HEADS * HEAD_PAD


def _latent_kernel(x_ref, g_ref, wl_ref, gc_ref, ga_ref, gb_ref, cos_ref, sin_ref,
                   wuk_ref, gkn_ref, wuvt_ref, ones_ref,
                   c_ref, kpe_ref, kf_ref, vt_ref):
    hn = _rms(x_ref[...], g_ref[...]).astype(BF16)
    y = _dot(hn, wl_ref[...])
    c = _rms(y[:, L_C0:L_A0], gc_ref[...])
    c_ref[...] = c
    a = y[:, L_A0:L_B0]
    b = y[:, L_B0:L_COLS]
    r = lax.rsqrt(jnp.sum(a * a, axis=-1, keepdims=True) * (1.0 / MLA_DR) + EPS)
    kpe = a * r * ga_ref[...] * cos_ref[...] + b * r * gb_ref[...] * sin_ref[...]
    kpe_ref[...] = kpe
    cb = c.astype(BF16)
    kn = _dot(cb, wuk_ref[...])
    ones = ones_ref[...]
    gkn = gkn_ref[...]
    for h in range(MLA_HEADS):
        hs = slice(h * HEAD_PAD, (h + 1) * HEAD_PAD)
        knh = kn[:, hs]
        ms = _group_sum(knh * knh, ones) * (1.0 / MLA_DN)
        kf_ref[:, hs] = (knh * lax.rsqrt(ms + EPS) * gkn + kpe).astype(BF16)
    vt = _dot_nt(wuvt_ref[...], cb)
    slab_row = lax.broadcasted_iota(jnp.int32, vt.shape, 0) % HEAD_PAD
    vt_ref[...] = jnp.where(slab_row == V_ONE, 1.0, vt).astype(BF16)


def _latent(x, g, wl, gc, ga, gb, cos, sin, wuk, gkn, wuvt, ones_h, tm):
    n = x.shape[0]
    row = lambda c: pl.BlockSpec((tm, c), lambda i: (i, 0))
    full = lambda a: pl.BlockSpec(a.shape, lambda i: (0,) * a.ndim)
    return pl.pallas_call(
        _latent_kernel,
        out_shape=(jax.ShapeDtypeStruct((n, MLA_KV_RANK), F32), jax.ShapeDtypeStruct((n, LANE), F32),
                   jax.ShapeDtypeStruct((n, KV_W), BF16), jax.ShapeDtypeStruct((KV_W, n), BF16)),
        grid=(n // tm,),
        in_specs=[row(D_MODEL), full(g), full(wl), full(gc), full(ga), full(gb), row(LANE), row(LANE),
                  full(wuk), full(gkn), full(wuvt), full(ones_h)],
        out_specs=(row(MLA_KV_RANK), row(LANE), row(KV_W), pl.BlockSpec((KV_W, tm), lambda i: (0, i))),
        compiler_params=_params(("parallel",)),
        name="shared_latent",
    )(x, g, wl, gc, ga, gb, cos, sin, wuk, gkn, wuvt, ones_h)


B_COLS = MLA_Q_RANK + MEM_W


def _b_in_kernel(x_ref, g_ref, w_ref, gqa_ref, w1_ref, w2_ref, g1_ref, g2_ref, inv_ref,
                 cos_ref, sin_ref, gq_ref, onesh_ref, ones64_ref, q_ref, mq_ref):
    u = _rms(x_ref[...], g_ref[...]).astype(BF16)
    p = _dot(u, w_ref[...])
    cq = _rms(p[:, :MLA_Q_RANK], gqa_ref[...]).astype(BF16)
    q1 = _dot(cq, w1_ref[...])
    q2 = _dot(cq, w2_ref[...])
    onesh = onesh_ref[...]
    inv = inv_ref[...]
    c1 = g1_ref[...] * cos_ref[...]
    c2 = g2_ref[...] * sin_ref[...]
    for h in range(MLA_HEADS):
        hs = slice(h * HEAD_PAD, (h + 1) * HEAD_PAD)
        q1h = q1[:, hs]
        rs = lax.rsqrt(_group_sum(q1h * q1h, onesh) * inv + EPS)
        q_ref[:, hs] = ((q1h * c1 + q2[:, hs] * c2) * rs).astype(BF16)
    mq_ref[...] = _head_rms(p[:, MLA_Q_RANK:B_COLS], ones64_ref[...], MEM_DH, gq_ref[...])


def _b_in(x, g, w, gqa, w1, w2, g1, g2, inv, cos, sin, gq, ones_h, ones64, tm):
    n = x.shape[0]
    row = lambda c: pl.BlockSpec((tm, c), lambda i: (i, 0))
    full = lambda a: pl.BlockSpec(a.shape, lambda i: (0,) * a.ndim)
    return pl.pallas_call(
        _b_in_kernel,
        out_shape=(jax.ShapeDtypeStruct((n, KV_W), BF16), jax.ShapeDtypeStruct((n, MEM_W), F32)),
        grid=(n // tm,),
        in_specs=[row(D_MODEL), full(g), full(w), full(gqa), full(w1), full(w2), full(g1), full(g2),
                  full(inv), row(LANE), row(LANE), full(gq), full(ones_h), full(ones64)],
        out_specs=(row(KV_W), row(MEM_W)),
        compiler_params=_params(("parallel",)),
        name="layer_b_in",
    )(x, g, w, gqa, w1, w2, g1, g2, inv, cos, sin, gq, ones_h, ones64)


def _flash_kernel(q_ref, k_ref, vt_ref, o_ref, sa_sc, sb_sc, m_sc, acc_sc):
    qi = pl.program_id(2)
    tq = q_ref.shape[1]
    tk = sa_sc.shape[0]
    q = q_ref[0]

    def scores_t(kb):
        start = pl.multiple_of(kb * tk, tk)
        return _dot_nt(k_ref[0, pl.ds(start, tk), :], q)

    def accumulate(s, kb, diagonal):
        if diagonal is not None:
            key = diagonal + lax.broadcasted_iota(jnp.int32, (tk, tq), 0)
            qry = lax.broadcasted_iota(jnp.int32, (tk, tq), 1)
            s = jnp.where(key <= qry, s, NEG)
        m = m_sc[...]
        m_new = jnp.maximum(m, jnp.max(s, axis=0, keepdims=True))
        p = jnp.exp2(s - m_new).astype(BF16)
        start = pl.multiple_of(kb * tk, tk)
        acc_sc[...] = jnp.exp2(m - m_new) * acc_sc[...] + _dot(vt_ref[:, pl.ds(start, tk)], p)
        m_sc[...] = m_new

    m_sc[...] = jnp.full_like(m_sc, NEG)
    acc_sc[...] = jnp.zeros_like(acc_sc)
    sa_sc[...] = scores_t(0)

    def pair(i, carry):
        kb = 2 * i
        sb_sc[...] = scores_t(kb + 1)
        accumulate(sa_sc[...], kb, None)
        sa_sc[...] = scores_t(kb + 2)
        accumulate(sb_sc[...], kb + 1, None)
        return carry

    lax.fori_loop(0, qi, pair, 0)
    sb_sc[...] = scores_t(2 * qi + 1)
    accumulate(sa_sc[...], 2 * qi, 0)
    accumulate(sb_sc[...], 2 * qi + 1, tk)
    acc = acc_sc[...]
    o_ref[0] = (acc / acc[V_ONE:V_ONE + 1, :]).T.astype(o_ref.dtype)


def _flash(q, k, vt, tq):
    b, t, _ = q.shape
    tk = tq // 2
    qspec = pl.BlockSpec((1, tq, HEAD_PAD), lambda bi, h, i: (bi, i, h))
    return pl.pallas_call(
        _flash_kernel,
        out_shape=jax.ShapeDtypeStruct((b, t, KV_W), BF16),
        grid=(b, MLA_HEADS, t // tq),
        in_specs=[qspec,
                  pl.BlockSpec((1, t, HEAD_PAD), lambda bi, h, i: (bi, 0, h)),
                  pl.BlockSpec((HEAD_PAD, t), lambda bi, h, i: (h, bi))],
        out_specs=qspec,
        scratch_shapes=[pltpu.VMEM((tk, tq), F32), pltpu.VMEM((tk, tq), F32),
                        pltpu.VMEM((1, tq), F32), pltpu.VMEM((HEAD_PAD, tq), F32)],
        compiler_params=_params(("parallel", "parallel", "arbitrary")),
        name="prompt_latent_attention",
    )(q, k, vt)


PAGES_PER_STEP = 64
QROWS = MLA_HEADS * T_PAD
SUB = 512


def _absorb_kernel(q_ref, w_ref, o_ref):
    o_ref[0] = _dot(q_ref[...], w_ref[0]).astype(BF16)


def _absorb(q, wabs):
    n = q.shape[0]
    return pl.pallas_call(
        _absorb_kernel,
        out_shape=jax.ShapeDtypeStruct((MLA_HEADS, n, MLA_KV_RANK), BF16),
        grid=(MLA_HEADS,),
        in_specs=[pl.BlockSpec((n, HEAD_PAD), lambda h: (0, h)),
                  pl.BlockSpec((1, HEAD_PAD, MLA_KV_RANK), lambda h: (h, 0, 0))],
        out_specs=pl.BlockSpec((1, n, MLA_KV_RANK), lambda h: (h, 0, 0)),
        compiler_params=_params(("parallel",)),
        name="absorb_w_uk",
    )(q, wabs)


def _scores(kt, sn, sp):
    rows = []
    for h in range(MLA_HEADS):
        kth = kt[h * MLA_DN:(h + 1) * MLA_DN, :]
        r = lax.rsqrt(jnp.sum(kth * kth, axis=0, keepdims=True) * (1.0 / MLA_DN) + EPS)
        rows.append(sn[h * T_PAD:(h + 1) * T_PAD, :] * r)
    return jnp.concatenate(rows, axis=0) + sp


def _online_softmax(s, cb, m, l, acc):
    m_new = jnp.maximum(m, jnp.max(s, axis=-1, keepdims=True))
    alpha = jnp.exp2(m - m_new)
    p = jnp.exp2(s - m_new)
    l = alpha * l + jnp.sum(p, axis=-1, keepdims=True)
    acc = alpha * acc + _dot(p.astype(BF16), cb)
    return m_new, l, acc


def _sample_attn_kernel(pt_ref, ckv_hbm, kpe_hbm, qa_ref, qp_ref, wukt_ref, cn_ref, kn_ref, o_ref,
                        cbuf, kbuf, csem, ksem, kt_sc, sn_sc, sp_sc, cb_sc, *, n_chunks, ch_pages, sub):
    b = pl.program_id(0)
    nb = pl.num_programs(0)
    page = cbuf.shape[1] // ch_pages
    n_sub = ch_pages * page // sub
    qa = qa_ref[0]
    qp = qp_ref[0]
    wukt = wukt_ref[...]

    def chunk_copies(seq, ch, slot):
        copies = []
        for i in range(ch_pages):
            pg = pt_ref[seq, ch * ch_pages + i]
            rows = pl.ds(i * page, page)
            copies.append(pltpu.make_async_copy(ckv_hbm.at[pg], cbuf.at[slot, rows, :], csem.at[slot]))
            copies.append(pltpu.make_async_copy(kpe_hbm.at[pg], kbuf.at[slot, :, rows], ksem.at[slot]))
        return copies

    def stage1(slot, i, buf):
        cb = cbuf[slot, i * sub:(i + 1) * sub, :].astype(BF16)
        cb_sc[buf] = cb
        kt_sc[buf] = _dot_nt(wukt, cb)
        sn_sc[buf] = _dot_nt(qa, cb)
        sp_sc[buf] = _dot(qp, kbuf[slot, :, i * sub:(i + 1) * sub].astype(BF16))

    def stage2(buf, state):
        return _online_softmax(_scores(kt_sc[buf], sn_sc[buf], sp_sc[buf]), cb_sc[buf], *state)

    @pl.when(b == 0)
    def _():
        for c in chunk_copies(0, 0, 0):
            c.start()

    cn = cn_ref[0].astype(BF16)
    s = _scores(_dot_nt(wukt, cn), _dot_nt(qa, cn), _dot(qp, kn_ref[0].astype(BF16)))
    t_q = lax.broadcasted_iota(jnp.int32, s.shape, 0) % T_PAD
    t_k = lax.broadcasted_iota(jnp.int32, s.shape, 1)
    init = (jnp.full((QROWS, 1), NEG, F32), jnp.zeros((QROWS, 1), F32), jnp.zeros((QROWS, MLA_KV_RANK), F32))
    state = _online_softmax(jnp.where(t_k <= t_q, s, NEG), cn, *init)

    def chunk(ch, slot, state):
        for c in chunk_copies(b, ch, slot):
            c.wait()
        in_seq = ch + 1 < n_chunks
        for c in chunk_copies(jnp.where(in_seq, b, (b + 1) % nb), jnp.where(in_seq, ch + 1, 0), 1 - slot):
            c.start()
        stage1(slot, 0, 0)
        for i in range(n_sub):
            if i + 1 < n_sub:
                stage1(slot, i + 1, (i + 1) % 2)
            state = stage2(i % 2, state)
        return state

    def chunk_pair(pi, state):
        return chunk(2 * pi + 1, 1, chunk(2 * pi, 0, state))

    _, l, acc = lax.fori_loop(0, n_chunks // 2, chunk_pair, state)
    @pl.when(b == nb - 1)
    def _():
        for c in chunk_copies(0, 0, 0):
            c.wait()

    o_ref[0] = acc / l


def _sample_attn(page_table, cache_ckv, cache_kpe_t, qa, qp, wukt, c_new, k_new_t):
    nb, n_pages = page_table.shape
    page = cache_ckv.shape[1]
    ch_pages = math.gcd(n_pages // 2, PAGES_PER_STEP)
    n_chunks = n_pages // ch_pages
    tokens = ch_pages * page
    sub = min(SUB, tokens // 2)
    assert tokens % (2 * sub) == 0 and n_chunks % 2 == 0
    per_b = lambda r, w: pl.BlockSpec((1, r, w), lambda b, pt: (b, 0, 0))
    hbm = pl.BlockSpec(memory_space=pl.ANY)
    return pl.pallas_call(
        functools.partial(_sample_attn_kernel, n_chunks=n_chunks, ch_pages=ch_pages, sub=sub),
        out_shape=jax.ShapeDtypeStruct((nb, QROWS, MLA_KV_RANK), F32),
        grid_spec=pltpu.PrefetchScalarGridSpec(
            num_scalar_prefetch=1,
            grid=(nb,),
            in_specs=[hbm, hbm, per_b(QROWS, MLA_KV_RANK), per_b(QROWS, MLA_DR),
                      pl.BlockSpec(wukt.shape, lambda b, pt: (0, 0)),
                      per_b(c_new.shape[1], MLA_KV_RANK), per_b(MLA_DR, k_new_t.shape[2])],
            out_specs=per_b(QROWS, MLA_KV_RANK),
            scratch_shapes=[pltpu.VMEM((2, tokens, MLA_KV_RANK), F32),
                            pltpu.VMEM((2, MLA_DR, tokens), F32),
                            pltpu.SemaphoreType.DMA((2,)), pltpu.SemaphoreType.DMA((2,)),
                            pltpu.VMEM((2, MLA_HEADS * MLA_DN, sub), F32),
                            pltpu.VMEM((2, QROWS, sub), F32),
                            pltpu.VMEM((2, QROWS, sub), F32),
                            pltpu.VMEM((2, sub, MLA_KV_RANK), BF16)],
        ),
        compiler_params=_params(("arbitrary",)),
        name="sample_latent_attention",
    )(page_table, cache_ckv, cache_kpe_t, qa, qp, wukt, c_new, k_new_t)


def _uv_kernel(x_ref, w_ref, o_ref):
    o_ref[0] = _dot(x_ref[0].astype(BF16), w_ref[0])


def _uv_project(ctx, wuv):
    _, n, _ = ctx.shape
    return pl.pallas_call(
        _uv_kernel,
        out_shape=jax.ShapeDtypeStruct((MLA_HEADS, n, HEAD_PAD), F32),
        grid=(MLA_HEADS,),
        in_specs=[pl.BlockSpec((1, n, MLA_KV_RANK), lambda h: (h, 0, 0)),
                  pl.BlockSpec((1, MLA_KV_RANK, HEAD_PAD), lambda h: (h, 0, 0))],
        out_specs=pl.BlockSpec((1, n, HEAD_PAD), lambda h: (h, 0, 0)),
        compiler_params=_params(("parallel",)),
        name="apply_w_uv",
    )(ctx, wuv)


def _memkv_kernel(m_ref, g_ref, wkt_ref, wvt_ref, gk_ref, ones_ref, kt_ref, vt_ref):
    mn = _rms(m_ref[0], g_ref[0]).astype(BF16)
    kt = _dot_nt(wkt_ref[0], mn)
    hi, mid, _ = _split3(kt * kt)
    ones = ones_ref[...]
    ms = (_dot(ones, hi) + _dot(ones, mid)) * (1.0 / MEM_DH)
    kt_ref[0, 0] = kt * lax.rsqrt(ms + EPS) * gk_ref[0]
    vt_ref[0, 0] = _dot_nt(wvt_ref[0], mn)


def _memkv(mem, g, wkt, wvt, gk, ones64):
    nb, m, _ = mem.shape
    nl = wkt.shape[0]
    lay = lambda a: pl.BlockSpec((1,) + a.shape[1:], lambda l, b: (l,) + (0,) * (a.ndim - 1))
    out = pl.BlockSpec((1, 1, MEM_W, m), lambda l, b: (l, b, 0, 0))
    shape = jax.ShapeDtypeStruct((nl, nb, MEM_W, m), F32)
    return pl.pallas_call(
        _memkv_kernel,
        out_shape=(shape, shape),
        grid=(nl, nb),
        in_specs=[pl.BlockSpec((1, m, D_MODEL), lambda l, b: (b, 0, 0)), lay(g), lay(wkt), lay(wvt), lay(gk),
                  pl.BlockSpec(ones64.shape, lambda l, b: (0, 0))],
        out_specs=(out, out),
        compiler_params=_params(("parallel", "parallel")),
        name="prompt_mem_kv",
    )(mem, g, wkt, wvt, gk, ones64)


def _pad_heads(w, heads, size, padded, axis=-1):
    axis = axis % w.ndim
    shape = w.shape[:axis] + (heads, size) + w.shape[axis + 1:]
    w = w.reshape(shape)
    pad = [(0, 0)] * w.ndim
    pad[axis + 1] = (0, padded - size)
    w = jnp.pad(w, pad)
    return w.reshape(w.shape[:axis] + (heads * padded,) + w.shape[axis + 2:])


def _block_ones(n, sizes):
    gid = []
    g = 0
    while len(gid) < n:
        for s in sizes:
            gid += [g] * s
            g += 1
    gid = jnp.asarray(gid[:n], jnp.int32)
    return (gid[:, None] == gid[None, :]).astype(BF16)


def _rope_tables(pos):
    half = MLA_DR // 2
    inv = ROPE_THETA ** (-jnp.arange(half, dtype=F32) / half)
    ang = pos.astype(F32)[:, None] * inv
    cos, sin = jnp.cos(ang), jnp.sin(ang)
    z = jnp.zeros((pos.shape[0], MLA_DN), F32)
    z2 = jnp.zeros((pos.shape[0], HEAD_PAD - MLA_DN - MLA_DR), F32)
    return (jnp.concatenate([z + 1.0, cos, cos, z2], axis=-1), jnp.concatenate([z, -sin, sin, z2], axis=-1))


def _swap_halves(w):
    half = w.shape[-1] // 2
    return jnp.concatenate([w[..., half:], w[..., :half]], axis=-1)


def _rope_slab(w):
    pad = [(0, 0)] * (w.ndim - 1) + [(MLA_DN, HEAD_PAD - MLA_DN - MLA_DR)]
    return jnp.pad(w, pad)


def _tile_for(n, pref):
    t = min(n, pref)
    while n % t:
        t //= 2
    return t


def kernel(x_prompt, x_sample, mem_prompt, cache_mem_k, cache_mem_v, state_gla, cache_ckv, cache_kpe, page_table,
           ffn1_norm, ffn1_w_gate, ffn1_w_up, ffn1_w_down, ffn2_norm, ffn2_w_gate, ffn2_w_up, ffn2_w_down,
           mix_norm, w_out, mem_norm, w_mem_k, w_mem_v, mem_k_norm, mem_q_norm,
           gla_w_in, gla_w_gate2, gla_b_gate, gla_o_norm,
           mla_w_in, mla_q_a_norm, mla_w_uq, mla_q_nope_norm, mla_q_pe_norm,
           kv_norm, kv_w_dkv, kv_ckv_norm, kv_w_kr, kv_kpe_norm, kv_w_uk, kv_k_nope_norm, kv_w_uv):
    bp, tp, _ = x_prompt.shape
    bs, ts, _ = x_sample.shape
    n_mem = mem_prompt.shape[1]
    past = page_table.shape[1] * cache_ckv.shape[1]
    row = lambda a: a.reshape(1, -1).astype(F32)
    bf = lambda a: a.astype(BF16)

    ffn = [[(row(n[l]), bf(g[l]), bf(u[l]), bf(d[l])) for l in range(2)]
           for n, g, u, d in ((ffn1_norm, ffn1_w_gate, ffn1_w_up, ffn1_w_down),
                              (ffn2_norm, ffn2_w_gate, ffn2_w_up, ffn2_w_down))]
    ones64 = _block_ones(MEM_W, [MEM_DH])
    ones_head = _block_ones(HEAD_PAD, [MLA_DN, MLA_DR, HEAD_PAD - MLA_DN - MLA_DR])
    gmq = [row(jnp.tile(mem_q_norm[l], MEM_HEADS)) for l in range(2)]

    wi = gla_w_in[0]
    o_q, o_k, o_v, o_g, o_r, o_m = 0, GLA_QK, 2 * GLA_QK, 2 * GLA_QK + GLA_V, 2 * GLA_QK + GLA_V + GLA_GATE_RANK, \
        2 * GLA_QK + 2 * GLA_V + GLA_GATE_RANK
    wa_in = bf(jnp.concatenate([
        _pad_heads(wi[:, o_q:o_k], GLA_HEADS, GLA_DK, DK_PAD),
        _pad_heads(wi[:, o_k:o_v], GLA_HEADS, GLA_DK, DK_PAD),
        _pad_heads(wi[:, o_v:o_g], GLA_HEADS, GLA_DV, DV_PAD),
        _pad_heads(wi[:, o_r:o_m], GLA_HEADS, GLA_DV, DV_PAD),
        wi[:, o_m:],
        jnp.pad(wi[:, o_g:o_r], ((0, 0), (0, LANE - GLA_GATE_RANK))),
    ], axis=1))
    w_gate2 = bf(jnp.pad(_pad_heads(gla_w_gate2[0], GLA_HEADS, GLA_DK, DK_PAD),
                         ((0, LANE - GLA_GATE_RANK), (0, 0))))
    b_gate = row(_pad_heads(gla_b_gate[0], GLA_HEADS, GLA_DK, DK_PAD))
    g_o = row(jnp.pad(gla_o_norm[0], (0, DV_PAD - GLA_DV)))
    g_o = jnp.tile(g_o, (1, GLA_HEADS))
    wa_out_main = bf(_pad_heads(w_out[0][:GLA_V], GLA_HEADS, GLA_DV, DV_PAD, axis=0))
    wa_out_mem = bf(w_out[0][GLA_V:])

    kr_slab = _rope_slab(kv_w_kr)
    w_lat = bf(jnp.concatenate([kv_w_dkv, kr_slab, _rope_slab(_swap_halves(kv_w_kr))], axis=1))
    g_kpe_a = row(_rope_slab(kv_kpe_norm))
    g_kpe_b = row(_rope_slab(_swap_halves(kv_kpe_norm)))
    w_uk_pad = bf(_pad_heads(kv_w_uk, MLA_HEADS, MLA_DN, HEAD_PAD))
    w_uv_t = bf(_pad_heads(kv_w_uv, MLA_HEADS, MLA_DV, HEAD_PAD).T)
    g_kn_slab = row(jnp.pad(kv_k_nope_norm, (0, HEAD_PAD - MLA_DN)))

    scale = (MLA_DN + MLA_DR) ** -0.5 * LOG2E
    wuq = mla_w_uq[0].reshape(MLA_Q_RANK, MLA_HEADS, MLA_DN + MLA_DR)
    w_q1 = bf(jnp.pad(wuq, ((0, 0), (0, 0), (0, HEAD_PAD - MLA_DN - MLA_DR))).reshape(MLA_Q_RANK, KV_W))
    w_q2 = bf(_rope_slab(_swap_halves(wuq[..., MLA_DN:])).reshape(MLA_Q_RANK, KV_W))
    g_q1 = row(jnp.concatenate([mla_q_nope_norm[0], mla_q_pe_norm[0],
                                jnp.zeros((HEAD_PAD - MLA_DN - MLA_DR,), F32)])) * scale
    g_q2 = row(_rope_slab(_swap_halves(mla_q_pe_norm[0]))) * scale
    inv_sizes = row(jnp.concatenate([jnp.full((MLA_DN,), 1.0 / MLA_DN, F32),
                                     jnp.full((HEAD_PAD - MLA_DN,), 1.0 / MLA_DR, F32)]))
    wb_in = bf(mla_w_in[0])
    wb_out_main = bf(_pad_heads(w_out[1][:MLA_HEADS * MLA_DV], MLA_HEADS, MLA_DV, HEAD_PAD, axis=0))
    wb_out_mem = bf(w_out[1][MLA_HEADS * MLA_DV:])
    wuk_h = kv_w_uk.reshape(MLA_KV_RANK, MLA_HEADS, MLA_DN).transpose(1, 2, 0)
    w_abs = bf(jnp.pad(wuk_h * kv_k_nope_norm[None, :, None], ((0, 0), (0, HEAD_PAD - MLA_DN), (0, 0))))
    w_uk_t = bf(kv_w_uk.T)
    w_uv_h = bf(jnp.pad(kv_w_uv.reshape(MLA_KV_RANK, MLA_HEADS, MLA_DV).transpose(1, 0, 2),
                        ((0, 0), (0, 0), (0, HEAD_PAD - MLA_DV))))

    g_mem_k = jnp.broadcast_to(jnp.tile(mem_k_norm, (1, MEM_HEADS))[:, :, None], (2, MEM_W, n_mem))
    mem_kt_p, mem_vt_p = _memkv(mem_prompt, mem_norm.reshape(2, 1, D_MODEL),
                                bf(w_mem_k.swapaxes(1, 2)), bf(w_mem_v.swapaxes(1, 2)), g_mem_k, ones64)

    def mem_out(a):
        return a.reshape(2, bp, MEM_HEADS, MEM_DH, n_mem).transpose(0, 1, 4, 2, 3)

    def trunk(x, nb, t, pos, mem_k, mem_v, s0, attend):
        n = nb * t
        tm = _tile_for(n, FFN_ROWS)
        tp_ = _tile_for(n, PROJ_ROWS)
        ta = _tile_for(n, A_IN_ROWS)
        seq_pad = (-t) % T_PAD
        tpad = t + seq_pad

        def to_seq(a):
            a = a.reshape(nb, t, a.shape[-1])
            return jnp.pad(a, ((0, 0), (0, seq_pad), (0, 0))) if seq_pad else a

        def from_seq(a):
            return a[:, :t].reshape(n, a.shape[-1])

        cos, sin = _rope_tables(pos)
        cos = jnp.tile(cos, (nb, 1))
        sin = jnp.tile(sin, (nb, 1))

        def mem_part(mq, l):
            return from_seq(_mem_attn(to_seq(mq), mem_k, mem_v, l, _tile_for(tpad, 1024)))

        x = _ffn_half(x, *ffn[0][0], tm)
        q, k, v, la, r, mq = _a_in(x, row(mix_norm[0]), wa_in, w_gate2, b_gate, gmq[0], ones64, ta)
        s0t = jnp.pad(s0.transpose(0, 1, 3, 2), ((0, 0), (0, 0), (0, DV_PAD - GLA_DV), (0, DK_PAD - GLA_DK)))
        o, st = _gla(to_seq(q), to_seq(k), to_seq(v), to_seq(la), s0t, math.gcd(tpad, GLA_CHUNK))
        gla_state = st[:, :, :GLA_DV, :GLA_DK].transpose(0, 1, 3, 2)
        x = _a_out(x, from_seq(o), r, g_o, mem_part(mq, 0), wa_out_main, wa_out_mem, tp_)
        x = _ffn_half(x, *ffn[1][0], tm)
        c, kpe, k_full, v_full = _latent(x, row(kv_norm), w_lat, row(kv_ckv_norm), g_kpe_a, g_kpe_b, cos, sin,
                                         w_uk_pad, g_kn_slab, w_uv_t, ones_head, tp_)
        kpe = kpe[:, MLA_DN:MLA_DN + MLA_DR]
        x = _ffn_half(x, *ffn[0][1], tm)
        q_full, mq = _b_in(x, row(mix_norm[1]), wb_in, row(mla_q_a_norm[0]), w_q1, w_q2, g_q1, g_q2, inv_sizes,
                           cos, sin, gmq[1], ones_head, ones64, tp_)
        o_main = attend(q_full, k_full, v_full, c, kpe)
        x = _b_out(x, o_main, mem_part(mq, 1), wb_out_main, wb_out_mem, tp_)
        x = _ffn_half(x, *ffn[1][1], tm)
        return x, gla_state, c, kpe

    def attend_prompt(q_full, k_full, v_full, c, kpe):
        sh = (bp, tp, KV_W)
        o = _flash(q_full.reshape(sh), k_full.reshape(sh), v_full, _tile_for(tp, FLASH_ROWS))
        return o.reshape(bp * tp, KV_W)

    def attend_sample(q_full, k_full, v_full, c, kpe):
        n = bs * ts
        qa = _absorb(q_full, w_abs)

        def rows(a):
            a = a.reshape(MLA_HEADS, bs, ts, a.shape[-1]).transpose(1, 0, 2, 3)
            a = jnp.pad(a, ((0, 0), (0, 0), (0, T_PAD - ts), (0, 0)))
            return a.reshape(bs, QROWS, a.shape[-1])

        qp = q_full.reshape(n, MLA_HEADS, HEAD_PAD)[:, :, MLA_DN:MLA_DN + MLA_DR].transpose(1, 0, 2)
        page = cache_ckv.shape[1]
        c_new = jnp.pad(c.reshape(bs, ts, MLA_KV_RANK), ((0, 0), (0, page - ts), (0, 0)))
        k_new_t = jnp.pad(kpe.reshape(bs, ts, MLA_DR), ((0, 0), (0, page - ts), (0, 0))).swapaxes(1, 2)
        ctx = _sample_attn(page_table, cache_ckv, cache_kpe.swapaxes(1, 2), rows(qa), rows(qp), w_uk_t,
                           c_new, k_new_t)
        ctx = ctx.reshape(bs, MLA_HEADS, T_PAD, MLA_KV_RANK).transpose(1, 0, 2, 3)
        o = _uv_project(ctx.reshape(MLA_HEADS, bs * T_PAD, MLA_KV_RANK), w_uv_h)
        o = o.reshape(MLA_HEADS, bs, T_PAD, HEAD_PAD)[:, :, :ts].transpose(1, 2, 0, 3)
        return o.reshape(n, KV_W)

    mk_p = mem_kt_p.reshape(2 * bp, MEM_W, n_mem)
    mv_p = mem_vt_p.reshape(2 * bp, MEM_W, n_mem)
    s0_p = jnp.zeros((bp, GLA_HEADS, GLA_DK, GLA_DV), F32)
    y_p, st_p, c_p, kpe_p = trunk(x_prompt.reshape(bp * tp, D_MODEL), bp, tp, jnp.arange(tp), mk_p, mv_p,
                                  s0_p, attend_prompt)

    def mem_in(a):
        return a.transpose(0, 1, 3, 4, 2).reshape(2 * bs, MEM_W, n_mem)

    mk_s = mem_in(cache_mem_k)
    mv_s = mem_in(cache_mem_v)
    y_s, st_s, c_s, kpe_s = trunk(x_sample.reshape(bs * ts, D_MODEL), bs, ts, past + jnp.arange(ts), mk_s, mv_s,
                                  state_gla[0], attend_sample)

    return (y_p.reshape(bp, tp, D_MODEL), y_s.reshape(bs, ts, D_MODEL),
            st_p[None], st_s[None],
            c_p.reshape(bp, tp, MLA_KV_RANK), kpe_p.reshape(bp, tp, MLA_DR),
            c_s.reshape(bs, ts, MLA_KV_RANK), kpe_s.reshape(bs, ts, MLA_DR),
            mem_out(mem_kt_p), mem_out(mem_vt_p))
```

```python
import functools
import math

import jax
import jax.numpy as jnp
from jax import lax
from jax.experimental import pallas as pl
from jax.experimental.pallas import tpu as pltpu

F32 = jnp.float32
BF16 = jnp.bfloat16

D_MODEL = 1024
D_FF = 2816
GLA_HEADS = 4
GLA_DK = 96
GLA_DV = 192
GLA_QK = GLA_HEADS * GLA_DK
GLA_V = GLA_HEADS * GLA_DV
GLA_GATE_RANK = 16
GLA_GATE_TEMP = 16.0
MEM_HEADS = 4
MEM_DH = 64
MEM_W = MEM_HEADS * MEM_DH
MLA_HEADS = 12
MLA_DN = 64
MLA_DR = 32
MLA_DV = 64
MLA_KV_RANK = 256
MLA_Q_RANK = 384
ROPE_THETA = 10000.0
EPS = 1e-6

LANE = 128
DK_PAD = LANE
DV_PAD = 2 * LANE
HEAD_PAD = LANE
V_ONE = MLA_DV
LOG2E = math.log2(math.e)
GLA_CHUNK = 64
GLA_SUB = 8
GLA_SEQS = 2
T_PAD = 8
NEG = -1e30
VMEM_LIMIT = 56 * 1024 * 1024
FFN_ROWS = 1024
FLASH_ROWS = 1024
FFN_COLS = 256
PROJ_ROWS = 512
A_IN_ROWS = 256

NT_DIMS = (((1,), (1,)), ((), ()))
TN_DIMS = (((0,), (0,)), ((), ()))


def _dot(a, b):
    return jnp.dot(a, b, preferred_element_type=F32)


def _dot_nt(a, b):
    return lax.dot_general(a, b, NT_DIMS, preferred_element_type=F32)


def _dot_tn(a, b):
    return lax.dot_general(a, b, TN_DIMS, preferred_element_type=F32)


def _rms(x, g):
    return x * lax.rsqrt(jnp.mean(x * x, axis=-1, keepdims=True) + EPS) * g


def _split3(x):
    hi = x.astype(BF16)
    r1 = x - hi.astype(F32)
    mid = r1.astype(BF16)
    lo = (r1 - mid.astype(F32)).astype(BF16)
    return hi, mid, lo


def _group_sum(x, ones_bd):
    hi, mid, _ = _split3(x)
    return _dot(hi, ones_bd) + _dot(mid, ones_bd)


def _params(sem):
    return pltpu.CompilerParams(dimension_semantics=sem, vmem_limit_bytes=VMEM_LIMIT)


def _ffn_kernel(x_ref, g_ref, wg_ref, wu_ref, wd_ref, o_ref, act_sc):
    x = x_ref[...]
    h = _rms(x, g_ref[...]).astype(BF16)
    for f in range(D_FF // FFN_COLS):
        cols = slice(f * FFN_COLS, (f + 1) * FFN_COLS)
        gate = _dot(h, wg_ref[:, cols])
        up = _dot(h, wu_ref[:, cols])
        act_sc[:, cols] = (gate * jax.nn.sigmoid(gate) * up).astype(BF16)
    o_ref[...] = x + 0.5 * _dot(act_sc[...], wd_ref[...])


def _ffn_half(x, g, wg, wu, wd, tm):
    n = x.shape[0]
    resident = lambda a: pl.BlockSpec(a.shape, lambda i: (0,) * a.ndim, pipeline_mode=pl.Buffered(1))
    return pl.pallas_call(
        _ffn_kernel,
        out_shape=jax.ShapeDtypeStruct((n, D_MODEL), F32),
        grid=(n // tm,),
        in_specs=[pl.BlockSpec((tm, D_MODEL), lambda i: (i, 0)), resident(g), resident(wg), resident(wu),
                  resident(wd)],
        out_specs=pl.BlockSpec((tm, D_MODEL), lambda i: (i, 0)),
        scratch_shapes=[pltpu.VMEM((tm, D_FF), BF16)],
        compiler_params=_params(("parallel",)),
        name="ffn_half",
    )(x, g, wg, wu, wd)


A_Q0 = 0
A_K0 = A_Q0 + GLA_HEADS * DK_PAD
A_V0 = A_K0 + GLA_HEADS * DK_PAD
A_R0 = A_V0 + GLA_HEADS * DV_PAD
A_M0 = A_R0 + GLA_HEADS * DV_PAD
A_G0 = A_M0 + MEM_W
A_COLS = A_G0 + LANE


def _head_rms(x, ones_bd, size, g):
    return x * lax.rsqrt(_group_sum(x * x, ones_bd) * (1.0 / size) + EPS) * g


def _a_in_kernel(x_ref, g_ref, w_ref, w2_ref, b2_ref, gq_ref, ones_ref,
                 q_ref, k_ref, v_ref, la_ref, r_ref, mq_ref):
    u = _rms(x_ref[...], g_ref[...]).astype(BF16)
    p = _dot(u, w_ref[...])
    q_ref[...] = p[:, A_Q0:A_K0] * (GLA_DK ** -0.5)
    k_ref[...] = p[:, A_K0:A_V0]
    v_ref[...] = p[:, A_V0:A_R0].astype(BF16)
    r = p[:, A_R0:A_M0]
    r_ref[...] = (r * jax.nn.sigmoid(r)).astype(BF16)
    mq_ref[...] = _head_rms(p[:, A_M0:A_G0], ones_ref[...], MEM_DH, gq_ref[...])
    z = _dot(p[:, A_G0:A_COLS].astype(BF16), w2_ref[...]) + b2_ref[...]
    log_sig = jnp.minimum(z, 0.0) - jnp.log1p(jnp.exp(-jnp.abs(z)))
    la_ref[...] = log_sig * (1.0 / GLA_GATE_TEMP)


def _a_in(x, g, w, w2, b2, gq, ones64, tm):
    n = x.shape[0]
    qk = GLA_HEADS * DK_PAD
    vv = GLA_HEADS * DV_PAD
    row = lambda c: pl.BlockSpec((tm, c), lambda i: (i, 0))
    full = lambda a: pl.BlockSpec(a.shape, lambda i: (0,) * a.ndim)
    return pl.pallas_call(
        _a_in_kernel,
        out_shape=(jax.ShapeDtypeStruct((n, qk), F32), jax.ShapeDtypeStruct((n, qk), F32),
                   jax.ShapeDtypeStruct((n, vv), BF16), jax.ShapeDtypeStruct((n, qk), F32),
                   jax.ShapeDtypeStruct((n, vv), BF16), jax.ShapeDtypeStruct((n, MEM_W), F32)),
        grid=(n // tm,),
        in_specs=[row(D_MODEL), full(g), full(w), full(w2), full(b2), full(gq), full(ones64)],
        out_specs=(row(qk), row(qk), row(vv), row(qk), row(vv), row(MEM_W)),
        compiler_params=_params(("parallel",)),
        name="layer_a_in",
    )(x, g, w, w2, b2, gq, ones64)


def _gla_kernel(q_ref, k_ref, v_ref, la_ref, s0_ref, o_ref, s_out_ref, st_sc):
    c = pl.program_id(1)
    nseq, chunk, _ = q_ref.shape
    sub = min(GLA_SUB, chunk)
    n_sc = chunk // sub

    @pl.when(c == 0)
    def _():
        st_sc[...] = s0_ref[...]

    t_idx = lax.broadcasted_iota(jnp.int32, (chunk, chunk), 0)
    s_idx = lax.broadcasted_iota(jnp.int32, (chunk, chunk), 1)
    tri = jnp.where(s_idx <= t_idx, 1.0, 0.0).astype(BF16)
    later = [(s_idx // sub == j) & (t_idx // sub > j) for j in range(n_sc - 1)]
    diag = [(s_idx == t_idx - d) & (t_idx % sub >= d) for d in range(sub)]
    for g in range(nseq):
        hi, mid, lo = _split3(la_ref[g])
        a_cum = _dot(tri, hi) + _dot(tri, mid) + _dot(tri, lo)
        a_last = a_cum[chunk - 1:chunk, :]
        a_end = jnp.concatenate(
            [jnp.broadcast_to(a_cum[(j + 1) * sub - 1:(j + 1) * sub, :], (sub, a_cum.shape[1]))
             for j in range(n_sc)], axis=0)
        e_q = jnp.exp(a_cum)
        e_kend = jnp.exp(a_end - a_cum)
        e_kd = jnp.exp(a_last - a_cum)
        e_last = jnp.exp(a_last)
        q = q_ref[g]
        k = k_ref[g]
        v = v_ref[g]
        for h in range(GLA_HEADS):
            ks = slice(h * DK_PAD, (h + 1) * DK_PAD)
            vs = slice(h * DV_PAD, (h + 1) * DV_PAD)
            a = a_cum[:, ks]
            qf = q[:, ks]
            kf = k[:, ks]
            k_end = (kf * e_kend[:, ks]).astype(BF16)
            scores = jnp.zeros((chunk, chunk), F32)
            for j in range(n_sc - 1):
                e_j = a[(j + 1) * sub - 1:(j + 1) * sub, :]
                q_j = (qf * jnp.exp(jnp.minimum(a - e_j, 0.0))).astype(BF16)
                scores = scores + jnp.where(later[j], _dot_nt(q_j, k_end), 0.0)
            for d in range(sub):
                k_d = kf if d == 0 else pltpu.roll(kf, d, 0)
                a_d = a if d == 0 else pltpu.roll(a, d, 0)
                pair = jnp.sum(qf * k_d * jnp.exp(jnp.minimum(a - a_d, 0.0)), axis=-1, keepdims=True)
                scores = scores + jnp.where(diag[d], pair, 0.0)
            qh = (qf * e_q[:, ks]).astype(BF16)
            kd = (kf * e_kd[:, ks]).astype(BF16)
            vh = v[:, vs]
            st = st_sc[g, h]
            o_ref[g, :, vs] = _dot(scores.astype(BF16), vh) + _dot_nt(qh, st.astype(BF16))
            st_sc[g, h] = st * e_last[:, ks] + _dot_tn(vh, kd)

    @pl.when(c == pl.num_programs(1) - 1)
    def _():
        s_out_ref[...] = st_sc[...]


def _gla(q, k, v, la, s0t, chunk):
    b, t, _ = q.shape
    qk = GLA_HEADS * DK_PAD
    vv = GLA_HEADS * DV_PAD
    nseq = math.gcd(b, GLA_SEQS)
    seq = lambda c: pl.BlockSpec((nseq, chunk, c), lambda i, j: (i, j, 0))
    st = pl.BlockSpec((nseq, GLA_HEADS, DV_PAD, DK_PAD), lambda i, j: (i, 0, 0, 0))
    return pl.pallas_call(
        _gla_kernel,
        out_shape=(jax.ShapeDtypeStruct((b, t, vv), F32),
                   jax.ShapeDtypeStruct((b, GLA_HEADS, DV_PAD, DK_PAD), F32)),
        grid=(b // nseq, t // chunk),
        in_specs=[seq(qk), seq(qk), seq(vv), seq(qk), st],
        out_specs=(seq(vv), st),
        scratch_shapes=[pltpu.VMEM((nseq, GLA_HEADS, DV_PAD, DK_PAD), F32)],
        compiler_params=_params(("parallel", "arbitrary")),
        name="gla_recurrence",
    )(q, k, v, la, s0t)


def _mem_attn_kernel(q_ref, kt_ref, vt_ref, o_ref):
    q = q_ref[0]
    vt = vt_ref[0].astype(BF16)
    head = lax.broadcasted_iota(jnp.int32, (1, MEM_W), 1) // MEM_DH
    if q.shape[0] >= LANE:
        k = kt_ref[0].T
        qb = q.astype(BF16)
        head_row = lax.broadcasted_iota(jnp.int32, (MEM_W, 1), 0) // MEM_DH
        acc_t = jnp.zeros((MEM_W, q.shape[0]), F32)
        for h in range(MEM_HEADS):
            kh = jnp.where(head == h, k, 0.0).astype(BF16)
            s = _dot_nt(kh, qb) * (MEM_DH ** -0.5 * LOG2E)
            p = jnp.exp2(s - jnp.max(s, axis=0, keepdims=True))
            pv = _dot(vt, p.astype(BF16)) / jnp.sum(p, axis=0, keepdims=True)
            acc_t = acc_t + jnp.where(head_row == h, pv, 0.0)
        o_ref[0] = acc_t.T
        return
    kt = kt_ref[0].astype(BF16)
    acc = jnp.zeros(q.shape, F32)
    for h in range(MEM_HEADS):
        sel = head == h
        qh = jnp.where(sel, q, 0.0).astype(BF16)
        s = _dot(qh, kt) * (MEM_DH ** -0.5)
        p = jnp.exp(s - jnp.max(s, axis=-1, keepdims=True))
        pv = _dot_nt(p.astype(BF16), vt) / jnp.sum(p, axis=-1, keepdims=True)
        acc = acc + jnp.where(sel, pv, 0.0)
    o_ref[0] = acc


def _mem_attn(q, mkt, mvt, layer, tq):
    b, t, _ = q.shape
    m = mkt.shape[2]
    base = layer * b
    return pl.pallas_call(
        _mem_attn_kernel,
        out_shape=jax.ShapeDtypeStruct((b, t, MEM_W), F32),
        grid=(b, t // tq),
        in_specs=[pl.BlockSpec((1, tq, MEM_W), lambda i, j: (i, j, 0)),
                  pl.BlockSpec((1, MEM_W, m), lambda i, j: (base + i, 0, 0)),
                  pl.BlockSpec((1, MEM_W, m), lambda i, j: (base + i, 0, 0))],
        out_specs=pl.BlockSpec((1, tq, MEM_W), lambda i, j: (i, j, 0)),
        compiler_params=_params(("parallel", "parallel")),
        name="mem_attention",
    )(q, mkt, mvt)


def _a_out_kernel(x_ref, o_ref, r_ref, go_ref, om_ref, wa_ref, wm_ref, y_ref):
    o = o_ref[...]
    parts = []
    for h in range(GLA_HEADS):
        oh = o[:, h * DV_PAD:(h + 1) * DV_PAD]
        ms = jnp.sum(oh * oh, axis=-1, keepdims=True) * (1.0 / GLA_DV)
        parts.append(oh * lax.rsqrt(ms + EPS))
    on = jnp.concatenate(parts, axis=-1) * go_ref[...]
    main = (on * r_ref[...]).astype(BF16)
    y_ref[...] = (x_ref[...] + _dot(main, wa_ref[...])
                  + _dot(om_ref[...].astype(BF16), wm_ref[...]))


def _a_out(x, o, r, go, om, wa, wm, tm):
    n = x.shape[0]
    row = lambda c: pl.BlockSpec((tm, c), lambda i: (i, 0))
    full = lambda a: pl.BlockSpec(a.shape, lambda i: (0,) * a.ndim)
    return pl.pallas_call(
        _a_out_kernel,
        out_shape=jax.ShapeDtypeStruct((n, D_MODEL), F32),
        grid=(n // tm,),
        in_specs=[row(D_MODEL), row(o.shape[1]), row(r.shape[1]), full(go), row(MEM_W),
                  full(wa), full(wm)],
        out_specs=row(D_MODEL),
        compiler_params=_params(("parallel",)),
        name="layer_a_out",
    )(x, o, r, go, om, wa, wm)


def _b_out_kernel(x_ref, o_ref, om_ref, wa_ref, wm_ref, y_ref):
    y_ref[...] = (x_ref[...] + _dot(o_ref[...].astype(BF16), wa_ref[...])
                  + _dot(om_ref[...].astype(BF16), wm_ref[...]))


def _b_out(x, o, om, wa, wm, tm):
    n = x.shape[0]
    row = lambda c: pl.BlockSpec((tm, c), lambda i: (i, 0))
    full = lambda a: pl.BlockSpec(a.shape, lambda i: (0,) * a.ndim)
    return pl.pallas_call(
        _b_out_kernel,
        out_shape=jax.ShapeDtypeStruct((n, D_MODEL), F32),
        grid=(n // tm,),
        in_specs=[row(D_MODEL), row(o.shape[1]), row(MEM_W), full(wa), full(wm)],
        out_specs=row(D_MODEL),
        compiler_params=_params(("parallel",)),
        name="layer_b_out",
    )(x, o, om, wa, wm)


L_C0 = 0
L_A0 = MLA_KV_RANK
L_B0 = L_A0 + LANE
L_COLS = L_B0 + LANE
KV_W = MLA_HEADS * HEAD_PAD


def _latent_kernel(x_ref, g_ref, wl_ref, gc_ref, ga_ref, gb_ref, cos_ref, sin_ref,
                   wuk_ref, gkn_ref, wuvt_ref, ones_ref,
                   c_ref, kpe_ref, kf_ref, vt_ref):
    hn = _rms(x_ref[...], g_ref[...]).astype(BF16)
    y = _dot(hn, wl_ref[...])
    c = _rms(y[:, L_C0:L_A0], gc_ref[...])
    c_ref[...] = c
    a = y[:, L_A0:L_B0]
    b = y[:, L_B0:L_COLS]
    r = lax.rsqrt(jnp.sum(a * a, axis=-1, keepdims=True) * (1.0 / MLA_DR) + EPS)
    kpe = a * r * ga_ref[...] * cos_ref[...] + b * r * gb_ref[...] * sin_ref[...]
    kpe_ref[...] = kpe
    cb = c.astype(BF16)
    kn = _dot(cb, wuk_ref[...])
    ones = ones_ref[...]
    gkn = gkn_ref[...]
    for h in range(MLA_HEADS):
        hs = slice(h * HEAD_PAD, (h + 1) * HEAD_PAD)
        knh = kn[:, hs]
        ms = _group_sum(knh * knh, ones) * (1.0 / MLA_DN)
        kf_ref[:, hs] = (knh * lax.rsqrt(ms + EPS) * gkn + kpe).astype(BF16)
    vt = _dot_nt(wuvt_ref[...], cb)
    slab_row = lax.broadcasted_iota(jnp.int32, vt.shape, 0) % HEAD_PAD
    vt_ref[...] = jnp.where(slab_row == V_ONE, 1.0, vt).astype(BF16)


def _latent(x, g, wl, gc, ga, gb, cos, sin, wuk, gkn, wuvt, ones_h, tm):
    n = x.shape[0]
    row = lambda c: pl.BlockSpec((tm, c), lambda i: (i, 0))
    full = lambda a: pl.BlockSpec(a.shape, lambda i: (0,) * a.ndim)
    return pl.pallas_call(
        _latent_kernel,
        out_shape=(jax.ShapeDtypeStruct((n, MLA_KV_RANK), F32), jax.ShapeDtypeStruct((n, LANE), F32),
                   jax.ShapeDtypeStruct((n, KV_W), BF16), jax.ShapeDtypeStruct((KV_W, n), BF16)),
        grid=(n // tm,),
        in_specs=[row(D_MODEL), full(g), full(wl), full(gc), full(ga), full(gb), row(LANE), row(LANE),
                  full(wuk), full(gkn), full(wuvt), full(ones_h)],
        out_specs=(row(MLA_KV_RANK), row(LANE), row(KV_W), pl.BlockSpec((KV_W, tm), lambda i: (0, i))),
        compiler_params=_params(("parallel",)),
        name="shared_latent",
    )(x, g, wl, gc, ga, gb, cos, sin, wuk, gkn, wuvt, ones_h)


B_COLS = MLA_Q_RANK + MEM_W


def _b_in_kernel(x_ref, g_ref, w_ref, gqa_ref, w1_ref, w2_ref, g1_ref, g2_ref, inv_ref,
                 cos_ref, sin_ref, gq_ref, onesh_ref, ones64_ref, q_ref, mq_ref):
    u = _rms(x_ref[...], g_ref[...]).astype(BF16)
    p = _dot(u, w_ref[...])
    cq = _rms(p[:, :MLA_Q_RANK], gqa_ref[...]).astype(BF16)
    q1 = _dot(cq, w1_ref[...])
    q2 = _dot(cq, w2_ref[...])
    onesh = onesh_ref[...]
    inv = inv_ref[...]
    c1 = g1_ref[...] * cos_ref[...]
    c2 = g2_ref[...] * sin_ref[...]
    for h in range(MLA_HEADS):
        hs = slice(h * HEAD_PAD, (h + 1) * HEAD_PAD)
        q1h = q1[:, hs]
        rs = lax.rsqrt(_group_sum(q1h * q1h, onesh) * inv + EPS)
        q_ref[:, hs] = ((q1h * c1 + q2[:, hs] * c2) * rs).astype(BF16)
    mq_ref[...] = _head_rms(p[:, MLA_Q_RANK:B_COLS], ones64_ref[...], MEM_DH, gq_ref[...])


def _b_in(x, g, w, gqa, w1, w2, g1, g2, inv, cos, sin, gq, ones_h, ones64, tm):
    n = x.shape[0]
    row = lambda c: pl.BlockSpec((tm, c), lambda i: (i, 0))
    full = lambda a: pl.BlockSpec(a.shape, lambda i: (0,) * a.ndim)
    return pl.pallas_call(
        _b_in_kernel,
        out_shape=(jax.ShapeDtypeStruct((n, KV_W), BF16), jax.ShapeDtypeStruct((n, MEM_W), F32)),
        grid=(n // tm,),
        in_specs=[row(D_MODEL), full(g), full(w), full(gqa), full(w1), full(w2), full(g1), full(g2),
                  full(inv), row(LANE), row(LANE), full(gq), full(ones_h), full(ones64)],
        out_specs=(row(KV_W), row(MEM_W)),
        compiler_params=_params(("parallel",)),
        name="layer_b_in",
    )(x, g, w, gqa, w1, w2, g1, g2, inv, cos, sin, gq, ones_h, ones64)


def _flash_kernel(q_ref, k_ref, vt_ref, o_ref, sa_sc, sb_sc, m_sc, acc_sc):
    qi = pl.program_id(2)
    tq = q_ref.shape[1]
    tk = sa_sc.shape[0]
    q = q_ref[0]

    def scores_t(kb):
        start = pl.multiple_of(kb * tk, tk)
        return _dot_nt(k_ref[0, pl.ds(start, tk), :], q)

    def accumulate(s, kb, diagonal):
        if diagonal is not None:
            key = diagonal + lax.broadcasted_iota(jnp.int32, (tk, tq), 0)
            qry = lax.broadcasted_iota(jnp.int32, (tk, tq), 1)
            s = jnp.where(key <= qry, s, NEG)
        m = m_sc[...]
        m_new = jnp.maximum(m, jnp.max(s, axis=0, keepdims=True))
        p = jnp.exp2(s - m_new).astype(BF16)
        start = pl.multiple_of(kb * tk, tk)
        acc_sc[...] = jnp.exp2(m - m_new) * acc_sc[...] + _dot(vt_ref[:, pl.ds(start, tk)], p)
        m_sc[...] = m_new

    m_sc[...] = jnp.full_like(m_sc, NEG)
    acc_sc[...] = jnp.zeros_like(acc_sc)
    sa_sc[...] = scores_t(0)

    def pair(i, carry):
        kb = 2 * i
        sb_sc[...] = scores_t(kb + 1)
        accumulate(sa_sc[...], kb, None)
        sa_sc[...] = scores_t(kb + 2)
        accumulate(sb_sc[...], kb + 1, None)
        return carry

    lax.fori_loop(0, qi, pair, 0)
    sb_sc[...] = scores_t(2 * qi + 1)
    accumulate(sa_sc[...], 2 * qi, 0)
    accumulate(sb_sc[...], 2 * qi + 1, tk)
    acc = acc_sc[...]
    o_ref[0] = (acc / acc[V_ONE:V_ONE + 1, :]).T.astype(o_ref.dtype)


def _flash(q, k, vt, tq):
    b, t, _ = q.shape
    tk = tq // 2
    qspec = pl.BlockSpec((1, tq, HEAD_PAD), lambda bi, h, i: (bi, i, h))
    return pl.pallas_call(
        _flash_kernel,
        out_shape=jax.ShapeDtypeStruct((b, t, KV_W), BF16),
        grid=(b, MLA_HEADS, t // tq),
        in_specs=[qspec,
                  pl.BlockSpec((1, t, HEAD_PAD), lambda bi, h, i: (bi, 0, h)),
                  pl.BlockSpec((HEAD_PAD, t), lambda bi, h, i: (h, bi))],
        out_specs=qspec,
        scratch_shapes=[pltpu.VMEM((tk, tq), F32), pltpu.VMEM((tk, tq), F32),
                        pltpu.VMEM((1, tq), F32), pltpu.VMEM((HEAD_PAD, tq), F32)],
        compiler_params=_params(("parallel", "parallel", "arbitrary")),
        name="prompt_latent_attention",
    )(q, k, vt)


PAGES_PER_STEP = 64
QROWS = MLA_HEADS * T_PAD
SUB = 512


def _absorb_kernel(q_ref, w_ref, o_ref):
    o_ref[0] = _dot(q_ref[...], w_ref[0]).astype(BF16)


def _absorb(q, wabs):
    n = q.shape[0]
    return pl.pallas_call(
        _absorb_kernel,
        out_shape=jax.ShapeDtypeStruct((MLA_HEADS, n, MLA_KV_RANK), BF16),
        grid=(MLA_HEADS,),
        in_specs=[pl.BlockSpec((n, HEAD_PAD), lambda h: (0, h)),
                  pl.BlockSpec((1, HEAD_PAD, MLA_KV_RANK), lambda h: (h, 0, 0))],
        out_specs=pl.BlockSpec((1, n, MLA_KV_RANK), lambda h: (h, 0, 0)),
        compiler_params=_params(("parallel",)),
        name="absorb_w_uk",
    )(q, wabs)


def _scores(kt, sn, sp):
    rows = []
    for h in range(MLA_HEADS):
        kth = kt[h * MLA_DN:(h + 1) * MLA_DN, :]
        r = lax.rsqrt(jnp.sum(kth * kth, axis=0, keepdims=True) * (1.0 / MLA_DN) + EPS)
        rows.append(sn[h * T_PAD:(h + 1) * T_PAD, :] * r)
    return jnp.concatenate(rows, axis=0) + sp


def _online_softmax(s, cb, m, l, acc):
    m_new = jnp.maximum(m, jnp.max(s, axis=-1, keepdims=True))
    alpha = jnp.exp2(m - m_new)
    p = jnp.exp2(s - m_new)
    l = alpha * l + jnp.sum(p, axis=-1, keepdims=True)
    acc = alpha * acc + _dot(p.astype(BF16), cb)
    return m_new, l, acc


def _sample_attn_kernel(pt_ref, ckv_hbm, kpe_hbm, qa_ref, qp_ref, wukt_ref, cn_ref, kn_ref, o_ref,
                        cbuf, kbuf, csem, ksem, kt_sc, sn_sc, sp_sc, cb_sc, *, n_chunks, ch_pages, sub):
    b = pl.program_id(0)
    nb = pl.num_programs(0)
    page = cbuf.shape[1] // ch_pages
    n_sub = ch_pages * page // sub
    qa = qa_ref[0]
    qp = qp_ref[0]
    wukt = wukt_ref[...]

    def chunk_copies(seq, ch, slot):
        copies = []
        for i in range(ch_pages):
            pg = pt_ref[seq, ch * ch_pages + i]
            rows = pl.ds(i * page, page)
            copies.append(pltpu.make_async_copy(ckv_hbm.at[pg], cbuf.at[slot, rows, :], csem.at[slot]))
            copies.append(pltpu.make_async_copy(kpe_hbm.at[pg], kbuf.at[slot, :, rows], ksem.at[slot]))
        return copies

    def stage1(slot, i, buf):
        cb = cbuf[slot, i * sub:(i + 1) * sub, :].astype(BF16)
        cb_sc[buf] = cb
        kt_sc[buf] = _dot_nt(wukt, cb)
        sn_sc[buf] = _dot_nt(qa, cb)
        sp_sc[buf] = _dot(qp, kbuf[slot, :, i * sub:(i + 1) * sub].astype(BF16))

    def stage2(buf, state):
        return _online_softmax(_scores(kt_sc[buf], sn_sc[buf], sp_sc[buf]), cb_sc[buf], *state)

    @pl.when(b == 0)
    def _():
        for c in chunk_copies(0, 0, 0):
            c.start()

    cn = cn_ref[0].astype(BF16)
    s = _scores(_dot_nt(wukt, cn), _dot_nt(qa, cn), _dot(qp, kn_ref[0].astype(BF16)))
    t_q = lax.broadcasted_iota(jnp.int32, s.shape, 0) % T_PAD
    t_k = lax.broadcasted_iota(jnp.int32, s.shape, 1)
    init = (jnp.full((QROWS, 1), NEG, F32), jnp.zeros((QROWS, 1), F32), jnp.zeros((QROWS, MLA_KV_RANK), F32))
    state = _online_softmax(jnp.where(t_k <= t_q, s, NEG), cn, *init)

    def chunk(ch, slot, state):
        for c in chunk_copies(b, ch, slot):
            c.wait()
        in_seq = ch + 1 < n_chunks
        for c in chunk_copies(jnp.where(in_seq, b, (b + 1) % nb), jnp.where(in_seq, ch + 1, 0), 1 - slot):
            c.start()
        stage1(slot, 0, 0)
        for i in range(n_sub):
            if i + 1 < n_sub:
                stage1(slot, i + 1, (i + 1) % 2)
            state = stage2(i % 2, state)
        return state

    def chunk_pair(pi, state):
        return chunk(2 * pi + 1, 1, chunk(2 * pi, 0, state))

    _, l, acc = lax.fori_loop(0, n_chunks // 2, chunk_pair, state)
    @pl.when(b == nb - 1)
    def _():
        for c in chunk_copies(0, 0, 0):
            c.wait()

    o_ref[0] = acc / l


def _sample_attn(page_table, cache_ckv, cache_kpe_t, qa, qp, wukt, c_new, k_new_t):
    nb, n_pages = page_table.shape
    page = cache_ckv.shape[1]
    ch_pages = math.gcd(n_pages // 2, PAGES_PER_STEP)
    n_chunks = n_pages // ch_pages
    tokens = ch_pages * page
    sub = min(SUB, tokens // 2)
    assert tokens % (2 * sub) == 0 and n_chunks % 2 == 0
    per_b = lambda r, w: pl.BlockSpec((1, r, w), lambda b, pt: (b, 0, 0))
    hbm = pl.BlockSpec(memory_space=pl.ANY)
    return pl.pallas_call(
        functools.partial(_sample_attn_kernel, n_chunks=n_chunks, ch_pages=ch_pages, sub=sub),
        out_shape=jax.ShapeDtypeStruct((nb, QROWS, MLA_KV_RANK), F32),
        grid_spec=pltpu.PrefetchScalarGridSpec(
            num_scalar_prefetch=1,
            grid=(nb,),
            in_specs=[hbm, hbm, per_b(QROWS, MLA_KV_RANK), per_b(QROWS, MLA_DR),
                      pl.BlockSpec(wukt.shape, lambda b, pt: (0, 0)),
                      per_b(c_new.shape[1], MLA_KV_RANK), per_b(MLA_DR, k_new_t.shape[2])],
            out_specs=per_b(QROWS, MLA_KV_RANK),
            scratch_shapes=[pltpu.VMEM((2, tokens, MLA_KV_RANK), F32),
                            pltpu.VMEM((2, MLA_DR, tokens), F32),
                            pltpu.SemaphoreType.DMA((2,)), pltpu.SemaphoreType.DMA((2,)),
                            pltpu.VMEM((2, MLA_HEADS * MLA_DN, sub), F32),
                            pltpu.VMEM((2, QROWS, sub), F32),
                            pltpu.VMEM((2, QROWS, sub), F32),
                            pltpu.VMEM((2, sub, MLA_KV_RANK), BF16)],
        ),
        compiler_params=_params(("arbitrary",)),
        name="sample_latent_attention",
    )(page_table, cache_ckv, cache_kpe_t, qa, qp, wukt, c_new, k_new_t)


def _uv_kernel(x_ref, w_ref, o_ref):
    o_ref[0] = _dot(x_ref[0].astype(BF16), w_ref[0])


def _uv_project(ctx, wuv):
    _, n, _ = ctx.shape
    return pl.pallas_call(
        _uv_kernel,
        out_shape=jax.ShapeDtypeStruct((MLA_HEADS, n, HEAD_PAD), F32),
        grid=(MLA_HEADS,),
        in_specs=[pl.BlockSpec((1, n, MLA_KV_RANK), lambda h: (h, 0, 0)),
                  pl.BlockSpec((1, MLA_KV_RANK, HEAD_PAD), lambda h: (h, 0, 0))],
        out_specs=pl.BlockSpec((1, n, HEAD_PAD), lambda h: (h, 0, 0)),
        compiler_params=_params(("parallel",)),
        name="apply_w_uv",
    )(ctx, wuv)


def _memkv_kernel(m_ref, g_ref, wkt_ref, wvt_ref, gk_ref, ones_ref, kt_ref, vt_ref):
    mn = _rms(m_ref[0], g_ref[0]).astype(BF16)
    kt = _dot_nt(wkt_ref[0], mn)
    hi, mid, _ = _split3(kt * kt)
    ones = ones_ref[...]
    ms = (_dot(ones, hi) + _dot(ones, mid)) * (1.0 / MEM_DH)
    kt_ref[0, 0] = kt * lax.rsqrt(ms + EPS) * gk_ref[0]
    vt_ref[0, 0] = _dot_nt(wvt_ref[0], mn)


def _memkv(mem, g, wkt, wvt, gk, ones64):
    nb, m, _ = mem.shape
    nl = wkt.shape[0]
    lay = lambda a: pl.BlockSpec((1,) + a.shape[1:], lambda l, b: (l,) + (0,) * (a.ndim - 1))
    out = pl.BlockSpec((1, 1, MEM_W, m), lambda l, b: (l, b, 0, 0))
    shape = jax.ShapeDtypeStruct((nl, nb, MEM_W, m), F32)
    return pl.pallas_call(
        _memkv_kernel,
        out_shape=(shape, shape),
        grid=(nl, nb),
        in_specs=[pl.BlockSpec((1, m, D_MODEL), lambda l, b: (b, 0, 0)), lay(g), lay(wkt), lay(wvt), lay(gk),
                  pl.BlockSpec(ones64.shape, lambda l, b: (0, 0))],
        out_specs=(out, out),
        compiler_params=_params(("parallel", "parallel")),
        name="prompt_mem_kv",
    )(mem, g, wkt, wvt, gk, ones64)


def _pad_heads(w, heads, size, padded, axis=-1):
    axis = axis % w.ndim
    shape = w.shape[:axis] + (heads, size) + w.shape[axis + 1:]
    w = w.reshape(shape)
    pad = [(0, 0)] * w.ndim
    pad[axis + 1] = (0, padded - size)
    w = jnp.pad(w, pad)
    return w.reshape(w.shape[:axis] + (heads * padded,) + w.shape[axis + 2:])


def _block_ones(n, sizes):
    gid = []
    g = 0
    while len(gid) < n:
        for s in sizes:
            gid += [g] * s
            g += 1
    gid = jnp.asarray(gid[:n], jnp.int32)
    return (gid[:, None] == gid[None, :]).astype(BF16)


def _rope_tables(pos):
    half = MLA_DR // 2
    inv = ROPE_THETA ** (-jnp.arange(half, dtype=F32) / half)
    ang = pos.astype(F32)[:, None] * inv
    cos, sin = jnp.cos(ang), jnp.sin(ang)
    z = jnp.zeros((pos.shape[0], MLA_DN), F32)
    z2 = jnp.zeros((pos.shape[0], HEAD_PAD - MLA_DN - MLA_DR), F32)
    return (jnp.concatenate([z + 1.0, cos, cos, z2], axis=-1), jnp.concatenate([z, -sin, sin, z2], axis=-1))


def _swap_halves(w):
    half = w.shape[-1] // 2
    return jnp.concatenate([w[..., half:], w[..., :half]], axis=-1)


def _rope_slab(w):
    pad = [(0, 0)] * (w.ndim - 1) + [(MLA_DN, HEAD_PAD - MLA_DN - MLA_DR)]
    return jnp.pad(w, pad)


def _tile_for(n, pref):
    t = min(n, pref)
    while n % t:
        t //= 2
    return t


def kernel(x_prompt, x_sample, mem_prompt, cache_mem_k, cache_mem_v, state_gla, cache_ckv, cache_kpe, page_table,
           ffn1_norm, ffn1_w_gate, ffn1_w_up, ffn1_w_down, ffn2_norm, ffn2_w_gate, ffn2_w_up, ffn2_w_down,
           mix_norm, w_out, mem_norm, w_mem_k, w_mem_v, mem_k_norm, mem_q_norm,
           gla_w_in, gla_w_gate2, gla_b_gate, gla_o_norm,
           mla_w_in, mla_q_a_norm, mla_w_uq, mla_q_nope_norm, mla_q_pe_norm,
           kv_norm, kv_w_dkv, kv_ckv_norm, kv_w_kr, kv_kpe_norm, kv_w_uk, kv_k_nope_norm, kv_w_uv):
    bp, tp, _ = x_prompt.shape
    bs, ts, _ = x_sample.shape
    n_mem = mem_prompt.shape[1]
    past = page_table.shape[1] * cache_ckv.shape[1]
    row = lambda a: a.reshape(1, -1).astype(F32)
    bf = lambda a: a.astype(BF16)

    ffn = [[(row(n[l]), bf(g[l]), bf(u[l]), bf(d[l])) for l in range(2)]
           for n, g, u, d in ((ffn1_norm, ffn1_w_gate, ffn1_w_up, ffn1_w_down),
                              (ffn2_norm, ffn2_w_gate, ffn2_w_up, ffn2_w_down))]
    ones64 = _block_ones(MEM_W, [MEM_DH])
    ones_head = _block_ones(HEAD_PAD, [MLA_DN, MLA_DR, HEAD_PAD - MLA_DN - MLA_DR])
    gmq = [row(jnp.tile(mem_q_norm[l], MEM_HEADS)) for l in range(2)]

    wi = gla_w_in[0]
    o_q, o_k, o_v, o_g, o_r, o_m = 0, GLA_QK, 2 * GLA_QK, 2 * GLA_QK + GLA_V, 2 * GLA_QK + GLA_V + GLA_GATE_RANK, \
        2 * GLA_QK + 2 * GLA_V + GLA_GATE_RANK
    wa_in = bf(jnp.concatenate([
        _pad_heads(wi[:, o_q:o_k], GLA_HEADS, GLA_DK, DK_PAD),
        _pad_heads(wi[:, o_k:o_v], GLA_HEADS, GLA_DK, DK_PAD),
        _pad_heads(wi[:, o_v:o_g], GLA_HEADS, GLA_DV, DV_PAD),
        _pad_heads(wi[:, o_r:o_m], GLA_HEADS, GLA_DV, DV_PAD),
        wi[:, o_m:],
        jnp.pad(wi[:, o_g:o_r], ((0, 0), (0, LANE - GLA_GATE_RANK))),
    ], axis=1))
    w_gate2 = bf(jnp.pad(_pad_heads(gla_w_gate2[0], GLA_HEADS, GLA_DK, DK_PAD),
                         ((0, LANE - GLA_GATE_RANK), (0, 0))))
    b_gate = row(_pad_heads(gla_b_gate[0], GLA_HEADS, GLA_DK, DK_PAD))
    g_o = row(jnp.pad(gla_o_norm[0], (0, DV_PAD - GLA_DV)))
    g_o = jnp.tile(g_o, (1, GLA_HEADS))
    wa_out_main = bf(_pad_heads(w_out[0][:GLA_V], GLA_HEADS, GLA_DV, DV_PAD, axis=0))
    wa_out_mem = bf(w_out[0][GLA_V:])

    kr_slab = _rope_slab(kv_w_kr)
    w_lat = bf(jnp.concatenate([kv_w_dkv, kr_slab, _rope_slab(_swap_halves(kv_w_kr))], axis=1))
    g_kpe_a = row(_rope_slab(kv_kpe_norm))
    g_kpe_b = row(_rope_slab(_swap_halves(kv_kpe_norm)))
    w_uk_pad = bf(_pad_heads(kv_w_uk, MLA_HEADS, MLA_DN, HEAD_PAD))
    w_uv_t = bf(_pad_heads(kv_w_uv, MLA_HEADS, MLA_DV, HEAD_PAD).T)
    g_kn_slab = row(jnp.pad(kv_k_nope_norm, (0, HEAD_PAD - MLA_DN)))

    scale = (MLA_DN + MLA_DR) ** -0.5 * LOG2E
    wuq = mla_w_uq[0].reshape(MLA_Q_RANK, MLA_HEADS, MLA_DN + MLA_DR)
    w_q1 = bf(jnp.pad(wuq, ((0, 0), (0, 0), (0, HEAD_PAD - MLA_DN - MLA_DR))).reshape(MLA_Q_RANK, KV_W))
    w_q2 = bf(_rope_slab(_swap_halves(wuq[..., MLA_DN:])).reshape(MLA_Q_RANK, KV_W))
    g_q1 = row(jnp.concatenate([mla_q_nope_norm[0], mla_q_pe_norm[0],
                                jnp.zeros((HEAD_PAD - MLA_DN - MLA_DR,), F32)])) * scale
    g_q2 = row(_rope_slab(_swap_halves(mla_q_pe_norm[0]))) * scale
    inv_sizes = row(jnp.concatenate([jnp.full((MLA_DN,), 1.0 / MLA_DN, F32),
                                     jnp.full((HEAD_PAD - MLA_DN,), 1.0 / MLA_DR, F32)]))
    wb_in = bf(mla_w_in[0])
    wb_out_main = bf(_pad_heads(w_out[1][:MLA_HEADS * MLA_DV], MLA_HEADS, MLA_DV, HEAD_PAD, axis=0))
    wb_out_mem = bf(w_out[1][MLA_HEADS * MLA_DV:])
    wuk_h = kv_w_uk.reshape(MLA_KV_RANK, MLA_HEADS, MLA_DN).transpose(1, 2, 0)
    w_abs = bf(jnp.pad(wuk_h * kv_k_nope_norm[None, :, None], ((0, 0), (0, HEAD_PAD - MLA_DN), (0, 0))))
    w_uk_t = bf(kv_w_uk.T)
    w_uv_h = bf(jnp.pad(kv_w_uv.reshape(MLA_KV_RANK, MLA_HEADS, MLA_DV).transpose(1, 0, 2),
                        ((0, 0), (0, 0), (0, HEAD_PAD - MLA_DV))))

    g_mem_k = jnp.broadcast_to(jnp.tile(mem_k_norm, (1, MEM_HEADS))[:, :, None], (2, MEM_W, n_mem))
    mem_kt_p, mem_vt_p = _memkv(mem_prompt, mem_norm.reshape(2, 1, D_MODEL),
                                bf(w_mem_k.swapaxes(1, 2)), bf(w_mem_v.swapaxes(1, 2)), g_mem_k, ones64)

    def mem_out(a):
        return a.reshape(2, bp, MEM_HEADS, MEM_DH, n_mem).transpose(0, 1, 4, 2, 3)

    def trunk(x, nb, t, pos, mem_k, mem_v, s0, attend):
        n = nb * t
        tm = _tile_for(n, FFN_ROWS)
        tp_ = _tile_for(n, PROJ_ROWS)
        ta = _tile_for(n, A_IN_ROWS)
        seq_pad = (-t) % T_PAD
        tpad = t + seq_pad

        def to_seq(a):
            a = a.reshape(nb, t, a.shape[-1])
            return jnp.pad(a, ((0, 0), (0, seq_pad), (0, 0))) if seq_pad else a

        def from_seq(a):
            return a[:, :t].reshape(n, a.shape[-1])

        cos, sin = _rope_tables(pos)
        cos = jnp.tile(cos, (nb, 1))
        sin = jnp.tile(sin, (nb, 1))

        def mem_part(mq, l):
            return from_seq(_mem_attn(to_seq(mq), mem_k, mem_v, l, _tile_for(tpad, 1024)))

        x = _ffn_half(x, *ffn[0][0], tm)
        q, k, v, la, r, mq = _a_in(x, row(mix_norm[0]), wa_in, w_gate2, b_gate, gmq[0], ones64, ta)
        s0t = jnp.pad(s0.transpose(0, 1, 3, 2), ((0, 0), (0, 0), (0, DV_PAD - GLA_DV), (0, DK_PAD - GLA_DK)))
        o, st = _gla(to_seq(q), to_seq(k), to_seq(v), to_seq(la), s0t, math.gcd(tpad, GLA_CHUNK))
        gla_state = st[:, :, :GLA_DV, :GLA_DK].transpose(0, 1, 3, 2)
        x = _a_out(x, from_seq(o), r, g_o, mem_part(mq, 0), wa_out_main, wa_out_mem, tp_)
        x = _ffn_half(x, *ffn[1][0], tm)
        c, kpe, k_full, v_full = _latent(x, row(kv_norm), w_lat, row(kv_ckv_norm), g_kpe_a, g_kpe_b, cos, sin,
                                         w_uk_pad, g_kn_slab, w_uv_t, ones_head, tp_)
        kpe = kpe[:, MLA_DN:MLA_DN + MLA_DR]
        x = _ffn_half(x, *ffn[0][1], tm)
        q_full, mq = _b_in(x, row(mix_norm[1]), wb_in, row(mla_q_a_norm[0]), w_q1, w_q2, g_q1, g_q2, inv_sizes,
                           cos, sin, gmq[1], ones_head, ones64, tp_)
        o_main = attend(q_full, k_full, v_full, c, kpe)
        x = _b_out(x, o_main, mem_part(mq, 1), wb_out_main, wb_out_mem, tp_)
        x = _ffn_half(x, *ffn[1][1], tm)
        return x, gla_state, c, kpe

    def attend_prompt(q_full, k_full, v_full, c, kpe):
        sh = (bp, tp, KV_W)
        o = _flash(q_full.reshape(sh), k_full.reshape(sh), v_full, _tile_for(tp, FLASH_ROWS))
        return o.reshape(bp * tp, KV_W)

    def attend_sample(q_full, k_full, v_full, c, kpe):
        n = bs * ts
        qa = _absorb(q_full, w_abs)

        def rows(a):
            a = a.reshape(MLA_HEADS, bs, ts, a.shape[-1]).transpose(1, 0, 2, 3)
            a = jnp.pad(a, ((0, 0), (0, 0), (0, T_PAD - ts), (0, 0)))
            return a.reshape(bs, QROWS, a.shape[-1])

        qp = q_full.reshape(n, MLA_HEADS, HEAD_PAD)[:, :, MLA_DN:MLA_DN + MLA_DR].transpose(1, 0, 2)
        page = cache_ckv.shape[1]
        c_new = jnp.pad(c.reshape(bs, ts, MLA_KV_RANK), ((0, 0), (0, page - ts), (0, 0)))
        k_new_t = jnp.pad(kpe.reshape(bs, ts, MLA_DR), ((0, 0), (0, page - ts), (0, 0))).swapaxes(1, 2)
        ctx = _sample_attn(page_table, cache_ckv, cache_kpe.swapaxes(1, 2), rows(qa), rows(qp), w_uk_t,
                           c_new, k_new_t)
        ctx = ctx.reshape(bs, MLA_HEADS, T_PAD, MLA_KV_RANK).transpose(1, 0, 2, 3)
        o = _uv_project(ctx.reshape(MLA_HEADS, bs * T_PAD, MLA_KV_RANK), w_uv_h)
        o = o.reshape(MLA_HEADS, bs, T_PAD, HEAD_PAD)[:, :, :ts].transpose(1, 2, 0, 3)
        return o.reshape(n, KV_W)

    mk_p = mem_kt_p.reshape(2 * bp, MEM_W, n_mem)
    mv_p = mem_vt_p.reshape(2 * bp, MEM_W, n_mem)
    s0_p = jnp.zeros((bp, GLA_HEADS, GLA_DK, GLA_DV), F32)
    y_p, st_p, c_p, kpe_p = trunk(x_prompt.reshape(bp * tp, D_MODEL), bp, tp, jnp.arange(tp), mk_p, mv_p,
                                  s0_p, attend_prompt)

    def mem_in(a):
        return a.transpose(0, 1, 3, 4, 2).reshape(2 * bs, MEM_W, n_mem)

    mk_s = mem_in(cache_mem_k)
    mv_s = mem_in(cache_mem_v)
    y_s, st_s, c_s, kpe_s = trunk(x_sample.reshape(bs * ts, D_MODEL), bs, ts, past + jnp.arange(ts), mk_s, mv_s,
                                  state_gla[0], attend_sample)

    return (y_p.reshape(bp, tp, D_MODEL), y_s.reshape(bs, ts, D_MODEL),
            st_p[None], st_s[None],
            c_p.reshape(bp, tp, MLA_KV_RANK), kpe_p.reshape(bp, tp, MLA_DR),
            c_s.reshape(bs, ts, MLA_KV_RANK), kpe_s.reshape(bs, ts, MLA_DR),
            mem_out(mem_kt_p), mem_out(mem_vt_p))
```
